```python
import math
import jax, jax.numpy as jnp
from jax import lax
import numpy as np

D_MODEL = 1024
BATCH = 8
SEQ = 2048
DEPTH = 2

CONV_CH = D_MODEL // 4
CONV_WIDTH = 31
MOBA_HEADS = 4
MOBA_HEAD_DIM = D_MODEL // 16
MOBA_WIDTH = MOBA_HEADS * MOBA_HEAD_DIM
MOBA_BLOCK = 256
MOBA_TOPK = 3
MOBA_Q_CHUNK = 64
GLA_HEADS = 4
GLA_WIDTH = D_MODEL // 2
GLA_DV = GLA_WIDTH // GLA_HEADS
GLA_DK = GLA_DV // 2
GLA_GATE_RANK = 16
GLA_TAU = 16.0
GLA_CHUNK = 32
MIX_WIDTH = CONV_CH + MOBA_WIDTH + GLA_WIDTH
D_FF = 2816
FFN_CONV_WIDTH = 3
NORM_EPS = 1e-6
NEG_INF = -1e30

COL_A = 2 * CONV_CH
COL_B = 3 * MOBA_WIDTH
COL_C_Q = GLA_HEADS * GLA_DK
COL_C_K = GLA_HEADS * GLA_DK
COL_C_V = GLA_WIDTH
COL_C_G = GLA_GATE_RANK
COL_C_R = GLA_WIDTH
COL_C = COL_C_Q + COL_C_K + COL_C_V + COL_C_G + COL_C_R
IN_COLS = COL_A + COL_B + COL_C

kernel_name = "hybrid_conformer_moba_gla_convffn"


def rms_norm(x, g, eps=NORM_EPS):
    xf = x.astype(jnp.float32)
    y = xf * lax.rsqrt(jnp.mean(xf * xf, axis=-1, keepdims=True) + eps)
    return (y * g.astype(jnp.float32)).astype(x.dtype)


def head_rms_norm(x, g, n_heads):
    B, S, W = x.shape
    xf = x.astype(jnp.float32).reshape(B, S, n_heads, W // n_heads)
    y = xf * lax.rsqrt(jnp.mean(xf * xf, axis=-1, keepdims=True) + NORM_EPS)
    return (y.reshape(B, S, W) * g.astype(jnp.float32)).astype(x.dtype)


def layer_norm(x, g, b, eps=1e-5):
    xf = x.astype(jnp.float32)
    mu = jnp.mean(xf, axis=-1, keepdims=True)
    var = jnp.mean(jnp.square(xf - mu), axis=-1, keepdims=True)
    y = (xf - mu) * lax.rsqrt(var + eps)
    return (y * g.astype(jnp.float32) + b.astype(jnp.float32)).astype(x.dtype)


def causal_depthwise_conv(x, w, b):
    W = w.shape[0]
    y = lax.conv_general_dilated(
        x, w[:, None, :].astype(x.dtype), window_strides=(1,), padding=[(W - 1, 0)],
        dimension_numbers=("NWC", "WIO", "NWC"), feature_group_count=x.shape[-1])
    return y + b.astype(x.dtype)


def conformer_conv(a_val, a_gate, conv_w, conv_b, ln_g, ln_b):
    h = a_val * jax.nn.sigmoid(a_gate)
    h = causal_depthwise_conv(h, conv_w, conv_b)
    h = layer_norm(h, ln_g, ln_b)
    return jax.nn.silu(h)


def moba_attention(q, k, v):
    B, S, H, dh = q.shape
    s_pad = ((S + MOBA_BLOCK - 1) // MOBA_BLOCK) * MOBA_BLOCK
    pad = [(0, 0), (0, s_pad - S), (0, 0), (0, 0)]
    qp = jnp.pad(q, pad).transpose(0, 2, 1, 3)
    kp = jnp.pad(k, pad).transpose(0, 2, 1, 3)
    vp = jnp.pad(v, pad).transpose(0, 2, 1, 3)
    nb = s_pad // MOBA_BLOCK
    topk = min(MOBA_TOPK, nb)
    kb = kp.reshape(B, H, nb, MOBA_BLOCK, dh)
    vb = vp.reshape(B, H, nb, MOBA_BLOCK, dh)
    k_mean = jnp.mean(kb.astype(jnp.float32), axis=3)
    nq = s_pad // MOBA_Q_CHUNK
    q_chunks = qp.reshape(B, H, nq, MOBA_Q_CHUNK, dh).transpose(2, 0, 1, 3, 4)
    scale = 1.0 / math.sqrt(dh)
    chunks_per_block = MOBA_BLOCK // MOBA_Q_CHUNK
    gather_blocks = jax.vmap(jax.vmap(lambda t, i: t[i]))

    def one_chunk(args):
        c, qc = args
        own = c // chunks_per_block
        q_pos = c * MOBA_Q_CHUNK + jnp.arange(MOBA_Q_CHUNK)
        gate = jnp.einsum("bhqd,bhnd->bhqn", qc.astype(jnp.float32), k_mean)
        gate = jnp.where(jnp.arange(nb) < own, gate, NEG_INF)
        _, idx = lax.top_k(gate, topk)
        valid = idx < own
        ks = gather_blocks(kb, idx)
        vs = gather_blocks(vb, idx)
        l_sel = jnp.einsum("bhqd,bhqtkd->bhqtk", qc, ks).astype(jnp.float32) * scale
        l_sel = jnp.where(valid[..., None], l_sel, NEG_INF).reshape(B, H, MOBA_Q_CHUNK, topk * MOBA_BLOCK)
        k_own = lax.dynamic_index_in_dim(kb, own, axis=2, keepdims=False)
        v_own = lax.dynamic_index_in_dim(vb, own, axis=2, keepdims=False)
        k_pos = own * MOBA_BLOCK + jnp.arange(MOBA_BLOCK)
        l_own = jnp.einsum("bhqd,bhkd->bhqk", qc, k_own).astype(jnp.float32) * scale
        l_own = jnp.where(k_pos[None, :] <= q_pos[:, None], l_own, NEG_INF)
        p = jax.nn.softmax(jnp.concatenate([l_sel, l_own], axis=-1), axis=-1)
        p_sel = p[..., : topk * MOBA_BLOCK].reshape(B, H, MOBA_Q_CHUNK, topk, MOBA_BLOCK).astype(vs.dtype)
        p_own = p[..., topk * MOBA_BLOCK:].astype(v_own.dtype)
        return (jnp.einsum("bhqtk,bhqtkd->bhqd", p_sel, vs)
                + jnp.einsum("bhqk,bhkd->bhqd", p_own, v_own))

    out = lax.map(one_chunk, (jnp.arange(nq), q_chunks))
    out = out.transpose(1, 0, 3, 2, 4).reshape(B, s_pad, H, dh)
    return out[:, :S].astype(q.dtype)


def gla_chunked(q, k, v, log_a):
    B, S, H, DK = q.shape
    DV = v.shape[-1]
    C = GLA_CHUNK
    nc = S // C

    def to_chunks(t):
        return t.astype(jnp.float32).reshape(B, nc, C, H, t.shape[-1]).transpose(0, 3, 1, 2, 4)

    qc, kc, vc, gc = to_chunks(q), to_chunks(k), to_chunks(v), to_chunks(log_a)
    G = jnp.cumsum(gc, axis=3)
    gamma = G[:, :, :, -1:, :]
    qg = qc * jnp.exp(G)
    kg = kc * jnp.exp(-G)
    kd = kc * jnp.exp(gamma - G)
    tril = jnp.tril(jnp.ones((C, C), dtype=bool))
    attn = jnp.where(tril, jnp.einsum("bhnid,bhnjd->bhnij", qg, kg), 0.0)
    o_intra = jnp.einsum("bhnij,bhnje->bhnie", attn, vc)
    d_state = jnp.einsum("bhnjd,bhnje->bhnde", kd, vc)
    decay = jnp.exp(gamma[:, :, :, 0, :])

    def step(state, inp):
        d, ds = inp
        return d[..., None] * state + ds, state

    _, s_prev = lax.scan(step, jnp.zeros((B, H, DK, DV), jnp.float32),
                         (jnp.moveaxis(decay, 2, 0), jnp.moveaxis(d_state, 2, 0)))
    s_prev = jnp.moveaxis(s_prev, 0, 2)
    o = o_intra + jnp.einsum("bhnid,bhnde->bhnie", qg, s_prev)
    return o.transpose(0, 2, 3, 1, 4).reshape(B, S, H, DV).astype(v.dtype)


def setup_inputs(seed: int = 0) -> dict:
    key = jax.random.key(seed)
    ks = jax.random.split(key, 20)
    f32 = jnp.float32
    n = lambda k, shape, s: jax.random.normal(k, shape, f32) * s
    return {
        "x": n(ks[0], (BATCH, SEQ, D_MODEL), 1.0),
        "norm_mix_g": 1.0 + n(ks[1], (DEPTH, D_MODEL), 0.05),
        "w_in": n(ks[2], (DEPTH, D_MODEL, IN_COLS), D_MODEL ** -0.5),
        "conv_w": n(ks[3], (DEPTH, CONV_WIDTH, CONV_CH), CONV_WIDTH ** -0.5),
        "conv_b": n(ks[4], (DEPTH, CONV_CH), 0.02),
        "conv_ln_g": 1.0 + n(ks[5], (DEPTH, CONV_CH), 0.05),
        "conv_ln_b": n(ks[6], (DEPTH, CONV_CH), 0.02),
        "moba_out_g": 1.0 + n(ks[7], (DEPTH, MOBA_WIDTH), 0.05),
        "gla_gate_w": n(ks[8], (DEPTH, GLA_GATE_RANK, GLA_HEADS * GLA_DK), GLA_GATE_RANK ** -0.5),
        "gla_gate_b": n(ks[9], (DEPTH, GLA_HEADS * GLA_DK), 0.1),
        "gla_out_g": 1.0 + n(ks[10], (DEPTH, GLA_WIDTH), 0.05),
        "w_out": n(ks[11], (DEPTH, MIX_WIDTH, D_MODEL), MIX_WIDTH ** -0.5),
        "norm_ffn_g": 1.0 + n(ks[12], (DEPTH, D_MODEL), 0.05),
        "ffn_w_up": n(ks[13], (DEPTH, D_MODEL, 2 * D_FF), D_MODEL ** -0.5),
        "ffn_conv_w": n(ks[14], (DEPTH, FFN_CONV_WIDTH, 2 * D_FF), FFN_CONV_WIDTH ** -0.5),
        "ffn_conv_b": n(ks[15], (DEPTH, 2 * D_FF), 0.02),
        "ffn_w_down": n(ks[16], (DEPTH, D_FF, D_MODEL), D_FF ** -0.5),
        "final_g": 1.0 + n(ks[17], (D_MODEL,), 0.05),
    }


def reference(x, norm_mix_g, w_in, conv_w, conv_b, conv_ln_g, conv_ln_b, moba_out_g,
              gla_gate_w, gla_gate_b, gla_out_g, w_out, norm_ffn_g, ffn_w_up, ffn_conv_w,
              ffn_conv_b, ffn_w_down, final_g):
    B, S, _ = x.shape
    o1 = COL_A
    o2 = o1 + COL_B
    o3 = o2 + COL_C_Q
    o4 = o3 + COL_C_K
    o5 = o4 + COL_C_V
    o6 = o5 + COL_C_G
    for l in range(DEPTH):
        h = rms_norm(x, norm_mix_g[l])
        p = h @ w_in[l].astype(h.dtype)
        y_a = conformer_conv(p[..., :CONV_CH], p[..., CONV_CH:o1],
                             conv_w[l], conv_b[l], conv_ln_g[l], conv_ln_b[l])
        qkv = p[..., o1:o2].reshape(B, S, 3, MOBA_HEADS, MOBA_HEAD_DIM)
        y_b = moba_attention(qkv[:, :, 0], qkv[:, :, 1], qkv[:, :, 2]).reshape(B, S, MOBA_WIDTH)
        y_b = head_rms_norm(y_b, moba_out_g[l], MOBA_HEADS)
        q_c = p[..., o2:o3].reshape(B, S, GLA_HEADS, GLA_DK) * (GLA_DK ** -0.5)
        k_c = p[..., o3:o4].reshape(B, S, GLA_HEADS, GLA_DK)
        v_c = p[..., o4:o5].reshape(B, S, GLA_HEADS, GLA_DV)
        gate_logit = (p[..., o5:o6] @ gla_gate_w[l].astype(p.dtype)).astype(jnp.float32) \
            + gla_gate_b[l].astype(jnp.float32)
        log_a = (jax.nn.log_sigmoid(gate_logit) / GLA_TAU).reshape(B, S, GLA_HEADS, GLA_DK)
        y_c = gla_chunked(q_c, k_c, v_c, log_a).reshape(B, S, GLA_WIDTH)
        y_c = head_rms_norm(y_c, gla_out_g[l], GLA_HEADS) * jax.nn.silu(p[..., o6:])
        y = jnp.concatenate([y_a, y_b, y_c], axis=-1) @ w_out[l].astype(x.dtype)
        x = x + y
        h = rms_norm(x, norm_ffn_g[l])
        u = causal_depthwise_conv(h @ ffn_w_up[l].astype(h.dtype), ffn_conv_w[l], ffn_conv_b[l])
        x = x + (jax.nn.silu(u[..., :D_FF]) * u[..., D_FF:]) @ ffn_w_down[l].astype(x.dtype)
    return rms_norm(x, final_g)
```

```python
import functools

import jax
import jax.numpy as jnp
from jax import lax
from jax.experimental import pallas as pl
from jax.experimental.pallas import tpu as pltpu

F32 = jnp.float32
BF16 = jnp.bfloat16

D_MODEL = 1024
CONV_CH = 256
CONV_WIDTH = 31
MOBA_HEADS = 4
MOBA_DH = 64
MOBA_WIDTH = 256
MOBA_BLOCK = 256
MOBA_TOPK = 3
GLA_HEADS = 4
GLA_DK = 64
GLA_DV = 128
GLA_WIDTH = 512
GLA_GATE_RANK = 16
GLA_TAU = 16.0
GLA_BLOCK = 256
GLA_SUB = 32
D_FF = 2816
FFN_CHUNK = 256
FFN_CONV_WIDTH = 3
NORM_EPS = 1e-6
LN_EPS = 1e-5
NEG_INF = -1e30
LANES = 128
SUBLANES = 8
VMEM_LIMIT = 56 * 1024 * 1024

_C_A = 0
_C_B = 512
_C_Q = 1280
_C_K = 1536
_C_V = 1792
_C_R = 2304
_C_G = 2816
_IN_COLS_PAD = 2944


def _sigmoid(x):
    return 1.0 / (1.0 + jnp.exp(-x))


def _rms(x, g):
    return x * lax.rsqrt(jnp.mean(x * x, axis=-1, keepdims=True) + NORM_EPS) * g


def _nt(a, b):
    return lax.dot_general(a, b, (((1,), (1,)), ((), ())), preferred_element_type=F32)


def _params(n_axes=1):
    return pltpu.CompilerParams(dimension_semantics=("arbitrary",) * n_axes,
                                vmem_limit_bytes=VMEM_LIMIT)


def _resident(shape):
    return pl.BlockSpec(shape, lambda *_: (0,) * len(shape), pipeline_mode=pl.Buffered(1))


def _in_proj_body(x_ref, g_ref, w_ref, gw_ref, gb_ref,
                  glu_ref, qkv_ref, gq_ref, gk_ref, gv_ref, gr_ref, la_ref):
    h = _rms(x_ref[...], g_ref[...]).astype(BF16)

    def mm(lo, hi):
        return jnp.dot(h, w_ref[:, lo:hi], preferred_element_type=F32)

    a = mm(_C_A, _C_B)
    glu_ref[...] = a[:, :CONV_CH] * _sigmoid(a[:, CONV_CH:])
    qkv_ref[...] = mm(_C_B, _C_Q).astype(BF16)
    gq_ref[...] = mm(_C_Q, _C_K) * (GLA_DK ** -0.5)
    gk_ref[...] = mm(_C_K, _C_V)
    gv_ref[...] = mm(_C_V, _C_R).astype(BF16)
    r = mm(_C_R, _C_G)
    gr_ref[...] = (r * _sigmoid(r)).astype(BF16)
    z = jnp.dot(mm(_C_G, _IN_COLS_PAD).astype(BF16), gw_ref[...],
                preferred_element_type=F32) + gb_ref[...]
    la_ref[...] = (jnp.minimum(z, 0.0) - jnp.log1p(jnp.exp(-jnp.abs(z)))) * (1.0 / GLA_TAU)


def _in_proj(x2, g, w, gw, gb, tm):
    t = x2.shape[0]
    row = lambda n: pl.BlockSpec((tm, n), lambda i: (i, 0))
    outs = [(CONV_CH, F32), (3 * MOBA_WIDTH, BF16), (GLA_HEADS * GLA_DK, F32), (GLA_HEADS * GLA_DK, F32),
            (GLA_WIDTH, BF16), (GLA_WIDTH, BF16), (GLA_HEADS * GLA_DK, F32)]
    return pl.pallas_call(
        _in_proj_body,
        grid=(t // tm,),
        in_specs=[row(D_MODEL), _resident((1, D_MODEL)), _resident(w.shape),
                  _resident(gw.shape), _resident(gb.shape)],
        out_specs=[row(n) for n, _ in outs],
        out_shape=[jax.ShapeDtypeStruct((t, n), d) for n, d in outs],
        compiler_params=_params(),
        name="in_proj",
    )(x2, g, w, gw, gb)


_CONV_PAD = 32
_CONV_ROWS = 64


def _conformer_body(glu_ref, w_ref, cb_ref, lg_ref, lb_ref, out_ref, hp_ref):
    seq = glu_ref.shape[0]
    halves = CONV_CH // LANES
    for c in range(halves):
        hp_ref[c, 0:_CONV_PAD, :] = jnp.zeros((_CONV_PAD, LANES), F32)
        hp_ref[c, _CONV_PAD:, :] = glu_ref[:, c * LANES:(c + 1) * LANES]

    def step(j, carry):
        r0 = pl.multiple_of(j * _CONV_ROWS, _CONV_ROWS)
        accs = []
        for c in range(halves):
            cols = slice(c * LANES, (c + 1) * LANES)
            acc = jnp.broadcast_to(cb_ref[:, cols], (_CONV_ROWS, LANES))
            for i in range(CONV_WIDTH):
                off = _CONV_PAD - (CONV_WIDTH - 1) + i
                acc = acc + w_ref[i:i + 1, cols] * hp_ref[c, pl.ds(r0 + off, _CONV_ROWS), :]
            accs.append(acc)
        y = jnp.concatenate(accs, axis=1)
        mu = jnp.mean(y, axis=-1, keepdims=True)
        d = y - mu
        var = jnp.mean(d * d, axis=-1, keepdims=True)
        y = d * lax.rsqrt(var + LN_EPS) * lg_ref[...] + lb_ref[...]
        out_ref[pl.ds(r0, _CONV_ROWS), :] = (y * _sigmoid(y)).astype(out_ref.dtype)
        return carry

    lax.fori_loop(0, seq // _CONV_ROWS, step, 0)


def _conformer(glu, w, cb, lg, lb, seq):
    t = glu.shape[0]
    return pl.pallas_call(
        _conformer_body,
        grid=(t // seq,),
        in_specs=[pl.BlockSpec((seq, CONV_CH), lambda b: (b, 0)), _resident(w.shape),
                  _resident(cb.shape), _resident(lg.shape), _resident(lb.shape)],
        out_specs=pl.BlockSpec((seq, CONV_CH), lambda b: (b, 0)),
        out_shape=jax.ShapeDtypeStruct((t, CONV_CH), BF16),
        scratch_shapes=[pltpu.VMEM((CONV_CH // LANES, _CONV_PAD + seq, LANES), F32)],
        compiler_params=_params(),
        name="conformer_conv",
    )(glu, w, cb, lg, lb)


def _moba_body(q_ref, k_ref, v_ref, g_ref, out_ref, kaug_ref):
    seq = q_ref.shape[0]
    nb = seq // MOBA_BLOCK
    blk = MOBA_BLOCK
    dh = MOBA_DH

    row_blk = lax.broadcasted_iota(jnp.int32, (seq, dh), 0) // blk
    lane = lax.broadcasted_iota(jnp.int32, (seq, dh), 1)
    onehot = jnp.where(lane == row_blk, 1.0, 0.0).astype(BF16)
    eye = jnp.where(lax.broadcasted_iota(jnp.int32, (blk, blk), 0)
                    == lax.broadcasted_iota(jnp.int32, (blk, blk), 1), 1.0, 0.0).astype(BF16)
    n_idx = lax.broadcasted_iota(jnp.int32, (nb, blk), 0)
    causal = (lax.broadcasted_iota(jnp.int32, (blk, blk), 1)
              <= lax.broadcasted_iota(jnp.int32, (blk, blk), 0))

    kmeans = []
    for h in range(MOBA_HEADS):
        kh = k_ref[:, h * dh:(h + 1) * dh]
        kaug_ref[h] = jnp.concatenate([kh, onehot], axis=1)
        km = jnp.sum(kh.astype(F32).reshape(nb, blk, dh), axis=1) * (1.0 / blk)
        km_hi = km.astype(BF16)
        km_lo = (km - km_hi.astype(F32)).astype(BF16)
        kmeans.append((km_hi, km_lo))

    for i in range(nb):
        rows = slice(i * blk, (i + 1) * blk)
        outs = []
        for h in range(MOBA_HEADS):
            qi = q_ref[rows, h * dh:(h + 1) * dh]
            km_hi, km_lo = kmeans[h]
            gate_t = _nt(km_hi, qi) + _nt(km_lo, qi)
            gm = jnp.where(n_idx < i, gate_t, NEG_INF)
            rank = jnp.zeros((nb, blk), F32)
            for m in range(nb):
                gm_m = gm[m:m + 1, :]
                beats = (gm_m > gm) | ((gm_m == gm) & (m < n_idx))
                rank = rank + jnp.where(beats, 1.0, 0.0)
            keep = ((rank < MOBA_TOPK) & (n_idx < i)) | (n_idx >= i)
            bias_t = jnp.where(keep, 0.0, NEG_INF)
            bias_t = jnp.concatenate([bias_t, jnp.zeros((dh - nb, blk), F32)], axis=0).astype(BF16)
            bias = _nt(eye, bias_t).astype(BF16)
            qaug = jnp.concatenate([qi * (MOBA_DH ** -0.5), bias], axis=1)

            s_own = _nt(qaug, kaug_ref[h, rows, :])
            s_own = jnp.where(causal, s_own, NEG_INF)
            m_row = jnp.max(s_own, axis=-1, keepdims=True)
            if i > 0:
                s_past = _nt(qaug, kaug_ref[h, 0:i * blk, :])
                m_row = jnp.maximum(m_row, jnp.max(s_past, axis=-1, keepdims=True))
            e_own = jnp.exp(s_own - m_row)
            l_row = jnp.sum(e_own, axis=-1, keepdims=True)
            o = jnp.dot(e_own.astype(BF16), v_ref[rows, h * dh:(h + 1) * dh],
                        preferred_element_type=F32)
            if i > 0:
                e_past = jnp.exp(s_past - m_row)
                l_row = l_row + jnp.sum(e_past, axis=-1, keepdims=True)
                o = o + jnp.dot(e_past.astype(BF16), v_ref[0:i * blk, h * dh:(h + 1) * dh],
                                preferred_element_type=F32)
            o = o / l_row
            o = o * lax.rsqrt(jnp.mean(o * o, axis=-1, keepdims=True) + NORM_EPS)
            outs.append(o)
        out_ref[rows, :] = (jnp.concatenate(outs, axis=1) * g_ref[...]).astype(out_ref.dtype)


def _moba(qkv, g, seq):
    t = qkv.shape[0]
    col = lambda c: pl.BlockSpec((seq, MOBA_WIDTH), lambda b: (b, c))
    return pl.pallas_call(
        _moba_body,
        grid=(t // seq,),
        in_specs=[col(0), col(1), col(2), _resident(g.shape)],
        out_specs=pl.BlockSpec((seq, MOBA_WIDTH), lambda b: (b, 0)),
        out_shape=jax.ShapeDtypeStruct((t, MOBA_WIDTH), BF16),
        scratch_shapes=[pltpu.VMEM((MOBA_HEADS, seq, 2 * MOBA_DH), BF16)],
        compiler_params=_params(),
        name="moba",
    )(qkv, qkv, qkv, g)


def _split3(x):
    hi = x.astype(BF16)
    r = x - hi.astype(F32)
    mid = r.astype(BF16)
    lo = (r - mid.astype(F32)).astype(BF16)
    return hi, mid, lo


def _gla_body(q_ref, k_ref, v_ref, la_ref, r_ref, g_ref, out_ref, st_ref):
    blk = GLA_BLOCK
    sub = GLA_SUB
    nsub = blk // sub
    dk = GLA_DK
    dv = GLA_DV

    @pl.when(pl.program_id(1) == 0)
    def _():
        st_ref[...] = jnp.zeros(st_ref.shape, F32)

    ri = lax.broadcasted_iota(jnp.int32, (blk, blk), 0)
    ci = lax.broadcasted_iota(jnp.int32, (blk, blk), 1)
    tril = jnp.where(ci <= ri, 1.0, 0.0).astype(BF16)
    causal = ci <= ri
    b_all = sum(jnp.dot(tril, part, preferred_element_type=F32) for part in _split3(la_ref[...]))
    rowi = lax.broadcasted_iota(jnp.int32, (blk, dk), 0)

    for h in range(GLA_HEADS):
        ks = slice(h * dk, (h + 1) * dk)
        vs = slice(h * dv, (h + 1) * dv)
        q = q_ref[:, ks]
        k = k_ref[:, ks]
        v = v_ref[:, vs]
        b = b_all[:, ks]
        s = [jnp.zeros((1, dk), F32)] + [b[I * sub - 1:I * sub, :] for I in range(1, nsub)]
        s_row = jnp.concatenate([jnp.broadcast_to(si, (sub, dk)) for si in s], axis=0)
        b_end = b[blk - 1:blk, :]
        qt = (q * jnp.exp(b - s_row)).astype(BF16)
        q_in = (q * jnp.exp(b)).astype(BF16)
        k_out = (k * jnp.exp(b_end - b)).astype(BF16)

        a_rows = []
        for I in range(nsub):
            arg = jnp.where(rowi < (I + 1) * sub, s[I] - b, NEG_INF)
            k_i = (k * jnp.exp(arg)).astype(BF16)
            a_rows.append(_nt(qt[I * sub:(I + 1) * sub, :], k_i))
        a = jnp.where(causal, jnp.concatenate(a_rows, axis=0), 0.0).astype(BF16)

        st = st_ref[h]
        o = jnp.dot(a, v, preferred_element_type=F32) + _nt(q_in, st.astype(BF16))
        v_t = v.astype(F32).T.astype(BF16)
        st_ref[h] = st * jnp.exp(b_end) + jnp.dot(v_t, k_out, preferred_element_type=F32)

        o = o * lax.rsqrt(jnp.mean(o * o, axis=-1, keepdims=True) + NORM_EPS) * g_ref[:, vs]
        out_ref[:, vs] = (o * r_ref[:, vs].astype(F32)).astype(out_ref.dtype)


def _gla(gq, gk, gv, la, gr, g, seq):
    t = gq.shape[0]
    nblk = seq // GLA_BLOCK
    blk = lambda n: pl.BlockSpec((GLA_BLOCK, n), lambda b, i: (b * nblk + i, 0))
    kw = GLA_HEADS * GLA_DK
    return pl.pallas_call(
        _gla_body,
        grid=(t // seq, nblk),
        in_specs=[blk(kw), blk(kw), blk(GLA_WIDTH), blk(kw), blk(GLA_WIDTH), _resident(g.shape)],
        out_specs=blk(GLA_WIDTH),
        out_shape=jax.ShapeDtypeStruct((t, GLA_WIDTH), BF16),
        scratch_shapes=[pltpu.VMEM((GLA_HEADS, GLA_DV, GLA_DK), F32)],
        compiler_params=_params(2),
        name="gla",
    )(gq, gk, gv, la, gr, g)


def _out_proj_body(x_ref, ya_ref, yb_ref, yc_ref, wa_ref, wb_ref, wc_ref, out_ref):
    y = jnp.dot(ya_ref[...], wa_ref[...], preferred_element_type=F32)
    y = y + jnp.dot(yb_ref[...], wb_ref[...], preferred_element_type=F32)
    y = y + jnp.dot(yc_ref[...], wc_ref[...], preferred_element_type=F32)
    out_ref[...] = x_ref[...] + y


def _out_proj(x2, ya, yb, yc, wa, wb, wc, tm):
    t = x2.shape[0]
    row = lambda n: pl.BlockSpec((tm, n), lambda i: (i, 0))
    return pl.pallas_call(
        _out_proj_body,
        grid=(t // tm,),
        in_specs=[row(D_MODEL), row(CONV_CH), row(MOBA_WIDTH), row(GLA_WIDTH),
                  _resident(wa.shape), _resident(wb.shape), _resident(wc.shape)],
        out_specs=row(D_MODEL),
        out_shape=jax.ShapeDtypeStruct((t, D_MODEL), F32),
        compiler_params=_params(),
        name="out_proj",
    )(x2, ya, yb, yc, wa, wb, wc)


def _ffn_body(x_ref, g_ref, wup_ref, cw_ref, cb_ref, wdn_ref, fg_ref, out_ref,
              acc_ref, ubuf_ref, carry_ref, *, tiles_per_seq, final_norm):
    tm = x_ref.shape[0]
    nch = D_FF // FFN_CHUNK
    halves = FFN_CHUNK // LANES
    taps = FFN_CONV_WIDTH

    @pl.when(pl.program_id(0) % tiles_per_seq == 0)
    def _():
        carry_ref[...] = jnp.zeros(carry_ref.shape, F32)

    x = x_ref[...]
    h = _rms(x, g_ref[...]).astype(BF16)
    acc_ref[...] = jnp.zeros(acc_ref.shape, F32)

    def conv_chunk(jj, slot):
        u = jnp.dot(h, wup_ref[jj], preferred_element_type=F32)
        cw = cw_ref[jj]
        outs = []
        for c in range(halves):
            cols = slice(c * LANES, (c + 1) * LANES)
            uc = u[:, cols]
            ubuf_ref[slot, c, 0:SUBLANES, :] = carry_ref[jj, :, cols]
            ubuf_ref[slot, c, SUBLANES:, :] = uc
            carry_ref[jj, :, cols] = uc[tm - SUBLANES:, :]
            y = cb_ref[jj][:, cols] + cw[taps - 1:taps, cols] * uc
            for d in range(1, taps):
                y = y + cw[taps - 1 - d:taps - d, cols] * ubuf_ref[slot, c, pl.ds(SUBLANES - d, tm), :]
            outs.append(y)
        return jnp.concatenate(outs, axis=1)

    def step(j, carry):
        val = conv_chunk(j, 0)
        gate = conv_chunk(nch + j, 1)
        a = (val * _sigmoid(val) * gate).astype(BF16)
        acc_ref[...] += jnp.dot(a, wdn_ref[j], preferred_element_type=F32)
        return carry

    lax.fori_loop(0, nch, step, 0)
    y = x + acc_ref[...]
    if final_norm:
        y = _rms(y, fg_ref[...])
    out_ref[...] = y


def _ffn(x2, g, wup, cw, cb, wdn, fg, tm, seq, final_norm):
    t = x2.shape[0]
    row = pl.BlockSpec((tm, D_MODEL), lambda i: (i, 0))
    nch2 = 2 * D_FF // FFN_CHUNK
    body = functools.partial(_ffn_body, tiles_per_seq=seq // tm, final_norm=final_norm)
    return pl.pallas_call(
        body,
        grid=(t // tm,),
        in_specs=[row, _resident(g.shape), _resident(wup.shape), _resident(cw.shape),
                  _resident(cb.shape), _resident(wdn.shape), _resident(fg.shape)],
        out_specs=row,
        out_shape=jax.ShapeDtypeStruct((t, D_MODEL), F32),
        scratch_shapes=[pltpu.VMEM((tm, D_MODEL), F32),
                        pltpu.VMEM((2, FFN_CHUNK // LANES, SUBLANES + tm, LANES), F32),
                        pltpu.VMEM((nch2, SUBLANES, FFN_CHUNK), F32)],
        compiler_params=_params(),
        name="ffn",
    )(x2, g, wup, cw, cb, wdn, fg)


def _pack_w_in(w):
    o_g = 2304
    o_r = o_g + GLA_GATE_RANK
    pad = jnp.zeros((D_MODEL, _IN_COLS_PAD - _C_G - GLA_GATE_RANK), w.dtype)
    return jnp.concatenate([w[:, :o_g], w[:, o_r:], w[:, o_g:o_r], pad], axis=1).astype(BF16)


def _chunk_cols(w, n):
    r, c = w.shape
    return w.reshape(r, c // n, n).transpose(1, 0, 2)


def kernel(x, norm_mix_g, w_in, conv_w, conv_b, conv_ln_g, conv_ln_b, moba_out_g,
           gla_gate_w, gla_gate_b, gla_out_g, w_out, norm_ffn_g, ffn_w_up, ffn_conv_w,
           ffn_conv_b, ffn_w_down, final_g):
    bsz, seq, d = x.shape
    depth = w_in.shape[0]
    x2 = x.reshape(bsz * seq, d)
    tm = 512
    row = lambda v: v.reshape(1, -1).astype(F32)
    for l in range(depth):
        w_in_p = _pack_w_in(w_in[l])
        gw = jnp.concatenate([gla_gate_w[l], jnp.zeros((LANES - GLA_GATE_RANK, GLA_HEADS * GLA_DK), F32)],
                             axis=0).astype(BF16)
        glu, qkv, gq, gk, gv, gr, la = _in_proj(x2, row(norm_mix_g[l]), w_in_p, gw, row(gla_gate_b[l]), tm)
        ya = _conformer(glu, conv_w[l], row(conv_b[l]), row(conv_ln_g[l]), row(conv_ln_b[l]), seq)
        yb = _moba(qkv, row(moba_out_g[l]), seq)
        yc = _gla(gq, gk, gv, la, gr, row(gla_out_g[l]), seq)
        wo = w_out[l].astype(BF16)
        x2 = _out_proj(x2, ya, yb, yc, wo[:CONV_CH], wo[CONV_CH:CONV_CH + MOBA_WIDTH],
                       wo[CONV_CH + MOBA_WIDTH:], tm)
        wup = _chunk_cols(ffn_w_up[l].astype(BF16), FFN_CHUNK)
        cw = _chunk_cols(ffn_conv_w[l], FFN_CHUNK)
        cb = _chunk_cols(ffn_conv_b[l].reshape(1, -1), FFN_CHUNK)
        wdn = ffn_w_down[l].astype(BF16).reshape(D_FF // FFN_CHUNK, FFN_CHUNK, D_MODEL)
        x2 = _ffn(x2, row(norm_ffn_g[l]), wup, cw, cb, wdn, row(final_g), tm, seq,
                  final_norm=(l == depth - 1))
    return x2.reshape(bsz, seq, d)
```

```python
import functools

import jax
import jax.numpy as jnp
from jax import lax
from jax.experimental import pallas as pl
from jax.experimental.pallas import tpu as pltpu

F32 = jnp.float32
BF16 = jnp.bfloat16

D_MODEL = 1024
CONV_CH = 256
CONV_WIDTH = 31
MOBA_HEADS = 4
MOBA_DH = 64
MOBA_WIDTH = 256
MOBA_BLOCK = 256
MOBA_TOPK = 3
GLA_HEADS = 4
GLA_DK = 64
GLA_DV = 128
GLA_WIDTH = 512
GLA_GATE_RANK = 16
GLA_TAU = 16.0
GLA_BLOCK = 256
GLA_SUB = 32
D_FF = 2816
FFN_CHUNK = 256
FFN_CONV_WIDTH = 3
NORM_EPS = 1e-6
LN_EPS = 1e-5
NEG_INF = -1e30
LANES = 128
SUBLANES = 8
VMEM_LIMIT = 56 * 1024 * 1024

_C_A = 0
_C_B = 512
_C_Q = 1280
_C_K = 1536
_C_V = 1792
_C_R = 2304
_C_G = 2816
_IN_COLS_PAD = 2944


def _sigmoid(x):
    return 1.0 / (1.0 + jnp.exp(-x))


def _rms(x, g):
    return x * lax.rsqrt(jnp.mean(x * x, axis=-1, keepdims=True) + NORM_EPS) * g


def _nt(a, b):
    return lax.dot_general(a, b, (((1,), (1,)), ((), ())), preferred_element_type=F32)


def _params(n_axes=1):
    return pltpu.CompilerParams(dimension_semantics=("arbitrary",) * n_axes,
                                vmem_limit_bytes=VMEM_LIMIT)


def _resident(shape):
    return pl.BlockSpec(shape, lambda *_: (0,) * len(shape), pipeline_mode=pl.Buffered(1))


def _in_proj_body(x_ref, g_ref, w_ref, gw_ref, gb_ref,
                  glu_ref, qkv_ref, gq_ref, gk_ref, gv_ref, gr_ref, la_ref):
    h = _rms(x_ref[...], g_ref[...]).astype(BF16)

    def mm(lo, hi):
        return jnp.dot(h, w_ref[:, lo:hi], preferred_element_type=F32)

    a = mm(_C_A, _C_B)
    glu_ref[...] = a[:, :CONV_CH] * _sigmoid(a[:, CONV_CH:])
    qkv_ref[...] = mm(_C_B, _C_Q).astype(BF16)
    gq_ref[...] = mm(_C_Q, _C_K) * (GLA_DK ** -0.5)
    gk_ref[...] = mm(_C_K, _C_V)
    gv_ref[...] = mm(_C_V, _C_R).astype(BF16)
    r = mm(_C_R, _C_G)
    gr_ref[...] = (r * _sigmoid(r)).astype(BF16)
    z = jnp.dot(mm(_C_G, _IN_COLS_PAD).astype(BF16), gw_ref[...],
                preferred_element_type=F32) + gb_ref[...]
    la_ref[...] = (jnp.minimum(z, 0.0) - jnp.log1p(jnp.exp(-jnp.abs(z)))) * (1.0 / GLA_TAU)


def _in_proj(x2, g, w, gw, gb, tm):
    t = x2.shape[0]
    row = lambda n: pl.BlockSpec((tm, n), lambda i: (i, 0))
    outs = [(CONV_CH, F32), (3 * MOBA_WIDTH, BF16), (GLA_HEADS * GLA_DK, F32), (GLA_HEADS * GLA_DK, F32),
            (GLA_WIDTH, BF16), (GLA_WIDTH, BF16), (GLA_HEADS * GLA_DK, F32)]
    return pl.pallas_call(
        _in_proj_body,
        grid=(t // tm,),
        in_specs=[row(D_MODEL), _resident((1, D_MODEL)), _resident(w.shape),
                  _resident(gw.shape), _resident(gb.shape)],
        out_specs=[row(n) for n, _ in outs],
        out_shape=[jax.ShapeDtypeStruct((t, n), d) for n, d in outs],
        compiler_params=_params(),
        name="in_proj",
    )(x2, g, w, gw, gb)


_CONV_PAD = 32
_CONV_ROWS = 64


def _conformer_body(glu_ref, w_ref, cb_ref, lg_ref, lb_ref, out_ref, hp_ref):
    seq = glu_ref.shape[0]
    halves = CONV_CH // LANES
    for c in range(halves):
        hp_ref[c, 0:_CONV_PAD, :] = jnp.zeros((_CONV_PAD, LANES), F32)
        hp_ref[c, _CONV_PAD:, :] = glu_ref[:, c * LANES:(c + 1) * LANES]

    def step(j, carry):
        r0 = pl.multiple_of(j * _CONV_ROWS, _CONV_ROWS)
        accs = []
        for c in range(halves):
            cols = slice(c * LANES, (c + 1) * LANES)
            acc = jnp.broadcast_to(cb_ref[:, cols], (_CONV_ROWS, LANES))
            for i in range(CONV_WIDTH):
                off = _CONV_PAD - (CONV_WIDTH - 1) + i
                acc = acc + w_ref[i:i + 1, cols] * hp_ref[c, pl.ds(r0 + off, _CONV_ROWS), :]
            accs.append(acc)
        y = jnp.concatenate(accs, axis=1)
        mu = jnp.mean(y, axis=-1, keepdims=True)
        d = y - mu
        var = jnp.mean(d * d, axis=-1, keepdims=True)
        y = d * lax.rsqrt(var + LN_EPS) * lg_ref[...] + lb_ref[...]
        out_ref[pl.ds(r0, _CONV_ROWS), :] = (y * _sigmoid(y)).astype(out_ref.dtype)
        return carry

    lax.fori_loop(0, seq // _CONV_ROWS, step, 0)


def _conformer(glu, w, cb, lg, lb, seq):
    t = glu.shape[0]
    return pl.pallas_call(
        _conformer_body,
        grid=(t // seq,),
        in_specs=[pl.BlockSpec((seq, CONV_CH), lambda b: (b, 0)), _resident(w.shape),
                  _resident(cb.shape), _resident(lg.shape), _resident(lb.shape)],
        out_specs=pl.BlockSpec((seq, CONV_CH), lambda b: (b, 0)),
        out_shape=jax.ShapeDtypeStruct((t, CONV_CH), BF16),
        scratch_shapes=[pltpu.VMEM((CONV_CH // LANES, _CONV_PAD + seq, LANES), F32)],
        compiler_params=_params(),
        name="conformer_conv",
    )(glu, w, cb, lg, lb)


def _moba_body(q_ref, k_ref, v_ref, g_ref, out_ref, kaug_ref):
    seq = q_ref.shape[0]
    nb = seq // MOBA_BLOCK
    blk = MOBA_BLOCK
    dh = MOBA_DH

    row_blk = lax.broadcasted_iota(jnp.int32, (seq, dh), 0) // blk
    lane = lax.broadcasted_iota(jnp.int32, (seq, dh), 1)
    onehot = jnp.where(lane == row_blk, 1.0, 0.0).astype(BF16)
    eye = jnp.where(lax.broadcasted_iota(jnp.int32, (blk, blk), 0)
                    == lax.broadcasted_iota(jnp.int32, (blk, blk), 1), 1.0, 0.0).astype(BF16)
    n_idx = lax.broadcasted_iota(jnp.int32, (nb, blk), 0)
    causal = (lax.broadcasted_iota(jnp.int32, (blk, blk), 1)
              <= lax.broadcasted_iota(jnp.int32, (blk, blk), 0))

    kmeans = []
    for h in range(MOBA_HEADS):
        kh = k_ref[:, h * dh:(h + 1) * dh]
        kaug_ref[h] = jnp.concatenate([kh, onehot], axis=1)
        km = jnp.sum(kh.astype(F32).reshape(nb, blk, dh), axis=1) * (1.0 / blk)
        km_hi = km.astype(BF16)
        km_lo = (km - km_hi.astype(F32)).astype(BF16)
        kmeans.append((km_hi, km_lo))

    for i in range(nb):
        rows = slice(i * blk, (i + 1) * blk)
        outs = []
        for h in range(MOBA_HEADS):
            qi = q_ref[rows, h * dh:(h + 1) * dh]
            km_hi, km_lo = kmeans[h]
            gate_t = _nt(km_hi, qi) + _nt(km_lo, qi)
            gm = jnp.where(n_idx < i, gate_t, NEG_INF)
            rank = jnp.zeros((nb, blk), F32)
            for m in range(nb):
                gm_m = gm[m:m + 1, :]
                beats = (gm_m > gm) | ((gm_m == gm) & (m < n_idx))
                rank = rank + jnp.where(beats, 1.0, 0.0)
            keep = ((rank < MOBA_TOPK) & (n_idx < i)) | (n_idx >= i)
            bias_t = jnp.where(keep, 0.0, NEG_INF)
            bias_t = jnp.concatenate([bias_t, jnp.zeros((dh - nb, blk), F32)], axis=0).astype(BF16)
            bias = _nt(eye, bias_t).astype(BF16)
            qaug = jnp.concatenate([qi * (MOBA_DH ** -0.5), bias], axis=1)

            s_own = _nt(qaug, kaug_ref[h, rows, :])
            s_own = jnp.where(causal, s_own, NEG_INF)
            m_row = jnp.max(s_own, axis=-1, keepdims=True)
            if i > 0:
                s_past = _nt(qaug, kaug_ref[h, 0:i * blk, :])
                m_row = jnp.maximum(m_row, jnp.max(s_past, axis=-1, keepdims=True))
            e_own = jnp.exp(s_own - m_row)
            l_row = jnp.sum(e_own, axis=-1, keepdims=True)
            o = jnp.dot(e_own.astype(BF16), v_ref[rows, h * dh:(h + 1) * dh],
                        preferred_element_type=F32)
            if i > 0:
                e_past = jnp.exp(s_past - m_row)
                l_row = l_row + jnp.sum(e_past, axis=-1, keepdims=True)
                o = o + jnp.dot(e_past.astype(BF16), v_ref[0:i * blk, h * dh:(h + 1) * dh],
                                preferred_element_type=F32)
            o = o / l_row
            o = o * lax.rsqrt(jnp.mean(o * o, axis=-1, keepdims=True) + NORM_EPS)
            outs.append(o)
        out_ref[rows, :] = (jnp.concatenate(outs, axis=1) * g_ref[...]).astype(out_ref.dtype)


def _moba(qkv, g, seq):
    t = qkv.shape[0]
    col = lambda c: pl.BlockSpec((seq, MOBA_WIDTH), lambda b: (b, c))
    return pl.pallas_call(
        _moba_body,
        grid=(t // seq,),
        in_specs=[col(0), col(1), col(2), _resident(g.shape)],
        out_specs=pl.BlockSpec((seq, MOBA_WIDTH), lambda b: (b, 0)),
        out_shape=jax.ShapeDtypeStruct((t, MOBA_WIDTH), BF16),
        scratch_shapes=[pltpu.VMEM((MOBA_HEADS, seq, 2 * MOBA_DH), BF16)],
        compiler_params=_params(),
        name="moba",
    )(qkv, qkv, qkv, g)


def _split3(x):
    hi = x.astype(BF16)
    r = x - hi.astype(F32)
    mid = r.astype(BF16)
    lo = (r - mid.astype(F32)).astype(BF16)
    return hi, mid, lo


def _gla_body(q_ref, k_ref, v_ref, la_ref, r_ref, g_ref, out_ref, st_ref):
    blk = GLA_BLOCK
    sub = GLA_SUB
    nsub = blk // sub
    dk = GLA_DK
    dv = GLA_DV

    @pl.when(pl.program_id(1) == 0)
    def _():
        st_ref[...] = jnp.zeros(st_ref.shape, F32)

    ri = lax.broadcasted_iota(jnp.int32, (blk, blk), 0)
    ci = lax.broadcasted_iota(jnp.int32, (blk, blk), 1)
    tril = jnp.where(ci <= ri, 1.0, 0.0).astype(BF16)
    causal = ci <= ri
    b_all = sum(jnp.dot(tril, part, preferred_element_type=F32) for part in _split3(la_ref[...]))
    rowi = lax.broadcasted_iota(jnp.int32, (blk, dk), 0)

    for h in range(GLA_HEADS):
        ks = slice(h * dk, (h + 1) * dk)
        vs = slice(h * dv, (h + 1) * dv)
        q = q_ref[:, ks]
        k = k_ref[:, ks]
        v = v_ref[:, vs]
        b = b_all[:, ks]
        s = [jnp.zeros((1, dk), F32)] + [b[I * sub - 1:I * sub, :] for I in range(1, nsub)]
        s_row = jnp.concatenate([jnp.broadcast_to(si, (sub, dk)) for si in s], axis=0)
        b_end = b[blk - 1:blk, :]
        qt = (q * jnp.exp(b - s_row)).astype(BF16)
        q_in = (q * jnp.exp(b)).astype(BF16)
        k_out = (k * jnp.exp(b_end - b)).astype(BF16)

        a_rows = []
        for I in range(nsub):
            arg = jnp.where(rowi < (I + 1) * sub, s[I] - b, NEG_INF)
            k_i = (k * jnp.exp(arg)).astype(BF16)
            a_rows.append(_nt(qt[I * sub:(I + 1) * sub, :], k_i))
        a = jnp.where(causal, jnp.concatenate(a_rows, axis=0), 0.0).astype(BF16)

        st = st_ref[h]
        o = jnp.dot(a, v, preferred_element_type=F32) + _nt(q_in, st.astype(BF16))
        v_t = v.astype(F32).T.astype(BF16)
        st_ref[h] = st * jnp.exp(b_end) + jnp.dot(v_t, k_out, preferred_element_type=F32)

        o = o * lax.rsqrt(jnp.mean(o * o, axis=-1, keepdims=True) + NORM_EPS) * g_ref[:, vs]
        out_ref[:, vs] = (o * r_ref[:, vs].astype(F32)).astype(out_ref.dtype)


def _gla(gq, gk, gv, la, gr, g, seq):
    t = gq.shape[0]
    nblk = seq // GLA_BLOCK
    blk = lambda n: pl.BlockSpec((GLA_BLOCK, n), lambda b, i: (b * nblk + i, 0))
    kw = GLA_HEADS * GLA_DK
    return pl.pallas_call(
        _gla_body,
        grid=(t // seq, nblk),
        in_specs=[blk(kw), blk(kw), blk(GLA_WIDTH), blk(kw), blk(GLA_WIDTH), _resident(g.shape)],
        out_specs=blk(GLA_WIDTH),
        out_shape=jax.ShapeDtypeStruct((t, GLA_WIDTH), BF16),
        scratch_shapes=[pltpu.VMEM((GLA_HEADS, GLA_DV, GLA_DK), F32)],
        compiler_params=_params(2),
        name="gla",
    )(gq, gk, gv, la, gr, g)


def _out_proj_body(x_ref, ya_ref, yb_ref, yc_ref, wa_ref, wb_ref, wc_ref, out_ref):
    y = jnp.dot(ya_ref[...], wa_ref[...], preferred_element_type=F32)
    y = y + jnp.dot(yb_ref[...], wb_ref[...], preferred_element_type=F32)
    y = y + jnp.dot(yc_ref[...], wc_ref[...], preferred_element_type=F32)
    out_ref[...] = x_ref[...] + y


def _out_proj(x2, ya, yb, yc, wa, wb, wc, tm):
    t = x2.shape[0]
    row = lambda n: pl.BlockSpec((tm, n), lambda i: (i, 0))
    return pl.pallas_call(
        _out_proj_body,
        grid=(t // tm,),
        in_specs=[row(D_MODEL), row(CONV_CH), row(MOBA_WIDTH), row(GLA_WIDTH),
                  _resident(wa.shape), _resident(wb.shape), _resident(wc.shape)],
        out_specs=row(D_MODEL),
        out_shape=jax.ShapeDtypeStruct((t, D_MODEL), F32),
        compiler_params=_params(),
        name="out_proj",
    )(x2, ya, yb, yc, wa, wb, wc)


def _ffn_body(x_ref, g_ref, wup_ref, cw_ref, cb_ref, wdn_ref, fg_ref, out_ref,
              h_ref, acc_ref, ubuf_a, ubuf_b, carry_ref, *, tiles_per_seq, final_norm):
    tm = x_ref.shape[0]
    nch = D_FF // FFN_CHUNK
    halves = FFN_CHUNK // LANES
    taps = FFN_CONV_WIDTH

    @pl.when(pl.program_id(0) % tiles_per_seq == 0)
    def _():
        carry_ref[...] = jnp.zeros(carry_ref.shape, F32)

    h_ref[...] = _rms(x_ref[...], g_ref[...]).astype(BF16)
    acc_ref[...] = jnp.zeros(acc_ref.shape, F32)

    def up(j, ubuf):
        for s in range(2):
            u = jnp.dot(h_ref[...], wup_ref[s * nch + j], preferred_element_type=F32)
            for c in range(halves):
                ubuf[s, c, SUBLANES:, :] = u[:, c * LANES:(c + 1) * LANES]

    def gate_down(j, ubuf):
        ys = []
        for s in range(2):
            jj = s * nch + j
            cw = cw_ref[jj]
            outs = []
            for c in range(halves):
                cols = slice(c * LANES, (c + 1) * LANES)
                ubuf[s, c, 0:SUBLANES, :] = carry_ref[jj, :, cols]
                y = cb_ref[jj][:, cols]
                for d in range(taps):
                    y = y + cw[taps - 1 - d:taps - d, cols] * ubuf[s, c, pl.ds(SUBLANES - d, tm), :]
                carry_ref[jj, :, cols] = ubuf[s, c, tm:tm + SUBLANES, :]
                outs.append(y)
            ys.append(jnp.concatenate(outs, axis=1))
        val, gate = ys
        a = (val * _sigmoid(val) * gate).astype(BF16)
        acc_ref[...] += jnp.dot(a, wdn_ref[j], preferred_element_type=F32)

    up(0, ubuf_a)

    def step(jp, carry):
        j = 2 * jp
        up(j + 1, ubuf_b)
        gate_down(j, ubuf_a)
        up(j + 2, ubuf_a)
        gate_down(j + 1, ubuf_b)
        return carry

    assert nch % 2 == 1
    lax.fori_loop(0, nch // 2, step, 0, unroll=True)
    gate_down(nch - 1, ubuf_a)
    y = x_ref[...] + acc_ref[...]
    if final_norm:
        y = _rms(y, fg_ref[...])
    out_ref[...] = y


def _ffn(x2, g, wup, cw, cb, wdn, fg, tm, seq, final_norm):
    t = x2.shape[0]
    row = pl.BlockSpec((tm, D_MODEL), lambda i: (i, 0))
    nch2 = 2 * D_FF // FFN_CHUNK
    body = functools.partial(_ffn_body, tiles_per_seq=seq // tm, final_norm=final_norm)
    return pl.pallas_call(
        body,
        grid=(t // tm,),
        in_specs=[row, _resident(g.shape), _resident(wup.shape), _resident(cw.shape),
                  _resident(cb.shape), _resident(wdn.shape), _resident(fg.shape)],
        out_specs=row,
        out_shape=jax.ShapeDtypeStruct((t, D_MODEL), F32),
        scratch_shapes=[pltpu.VMEM((tm, D_MODEL), BF16),
                        pltpu.VMEM((tm, D_MODEL), F32),
                        pltpu.VMEM((2, FFN_CHUNK // LANES, SUBLANES + tm, LANES), F32),
                        pltpu.VMEM((2, FFN_CHUNK // LANES, SUBLANES + tm, LANES), F32),
                        pltpu.VMEM((nch2, SUBLANES, FFN_CHUNK), F32)],
        compiler_params=_params(),
        name="ffn",
    )(x2, g, wup, cw, cb, wdn, fg)


def _pack_w_in(w):
    o_g = 2304
    o_r = o_g + GLA_GATE_RANK
    pad = jnp.zeros((D_MODEL, _IN_COLS_PAD - _C_G - GLA_GATE_RANK), w.dtype)
    return jnp.concatenate([w[:, :o_g], w[:, o_r:], w[:, o_g:o_r], pad], axis=1).astype(BF16)


def _chunk_cols(w, n):
    r, c = w.shape
    return w.reshape(r, c // n, n).transpose(1, 0, 2)


def kernel(x, norm_mix_g, w_in, conv_w, conv_b, conv_ln_g, conv_ln_b, moba_out_g,
           gla_gate_w, gla_gate_b, gla_out_g, w_out, norm_ffn_g, ffn_w_up, ffn_conv_w,
           ffn_conv_b, ffn_w_down, final_g):
    bsz, seq, d = x.shape
    depth = w_in.shape[0]
    x2 = x.reshape(bsz * seq, d)
    tm = 512
    row = lambda v: v.reshape(1, -1).astype(F32)
    for l in range(depth):
        w_in_p = _pack_w_in(w_in[l])
        gw = jnp.concatenate([gla_gate_w[l], jnp.zeros((LANES - GLA_GATE_RANK, GLA_HEADS * GLA_DK), F32)],
                             axis=0).astype(BF16)
        glu, qkv, gq, gk, gv, gr, la = _in_proj(x2, row(norm_mix_g[l]), w_in_p, gw, row(gla_gate_b[l]), tm)
        ya = _conformer(glu, conv_w[l], row(conv_b[l]), row(conv_ln_g[l]), row(conv_ln_b[l]), seq)
        yb = _moba(qkv, row(moba_out_g[l]), seq)
        yc = _gla(gq, gk, gv, la, gr, row(gla_out_g[l]), seq)
        wo = w_out[l].astype(BF16)
        x2 = _out_proj(x2, ya, yb, yc, wo[:CONV_CH], wo[CONV_CH:CONV_CH + MOBA_WIDTH],
                       wo[CONV_CH + MOBA_WIDTH:], tm)
        wup = _chunk_cols(ffn_w_up[l].astype(BF16), FFN_CHUNK)
        cw = _chunk_cols(ffn_conv_w[l], FFN_CHUNK)
        cb = _chunk_cols(ffn_conv_b[l].reshape(1, -1), FFN_CHUNK)
        wdn = ffn_w_down[l].astype(BF16).reshape(D_FF // FFN_CHUNK, FFN_CHUNK, D_MODEL)
        x2 = _ffn(x2, row(norm_ffn_g[l]), wup, cw, cb, wdn, row(final_g), tm, seq,
                  final_norm=(l == depth - 1))
    return x2.reshape(bsz, seq, d)
```

```python
import functools

import jax
import jax.numpy as jnp
from jax import lax
from jax.experimental import pallas as pl
from jax.experimental.pallas import tpu as pltpu

F32 = jnp.float32
BF16 = jnp.bfloat16

D_MODEL = 1024
CONV_CH = 256
CONV_WIDTH = 31
MOBA_HEADS = 4
MOBA_DH = 64
MOBA_WIDTH = 256
MOBA_BLOCK = 256
MOBA_TOPK = 3
GLA_HEADS = 4
GLA_DK = 64
GLA_DV = 128
GLA_WIDTH = 512
GLA_GATE_RANK = 16
GLA_TAU = 16.0
GLA_BLOCK = 256
GLA_SUB = 32
D_FF = 2816
FFN_CHUNK = 256
FFN_CONV_WIDTH = 3
NORM_EPS = 1e-6
LN_EPS = 1e-5
NEG_INF = -1e30
LANES = 128
SUBLANES = 8
VMEM_LIMIT = 56 * 1024 * 1024

_C_A = 0
_C_B = 512
_C_Q = 1280
_C_K = 1536
_C_V = 1792
_C_R = 2304
_C_G = 2816
_IN_COLS_PAD = 2944


def _sigmoid(x):
    return 1.0 / (1.0 + jnp.exp(-x))


def _rms(x, g):
    return x * lax.rsqrt(jnp.mean(x * x, axis=-1, keepdims=True) + NORM_EPS) * g


def _nt(a, b):
    return lax.dot_general(a, b, (((1,), (1,)), ((), ())), preferred_element_type=F32)


def _params(n_axes=1):
    return pltpu.CompilerParams(dimension_semantics=("arbitrary",) * n_axes,
                                vmem_limit_bytes=VMEM_LIMIT)


def _resident(shape):
    return pl.BlockSpec(shape, lambda *_: (0,) * len(shape), pipeline_mode=pl.Buffered(1))


def _in_proj_body(x_ref, g_ref, w_ref, gw_ref, gb_ref,
                  glu_ref, qkv_ref, gq_ref, gk_ref, gv_ref, gr_ref, la_ref):
    h = _rms(x_ref[...], g_ref[...]).astype(BF16)

    def mm(lo, hi):
        return jnp.dot(h, w_ref[:, lo:hi], preferred_element_type=F32)

    a = mm(_C_A, _C_B)
    glu_ref[...] = a[:, :CONV_CH] * _sigmoid(a[:, CONV_CH:])
    qkv_ref[...] = mm(_C_B, _C_Q).astype(BF16)
    gq_ref[...] = mm(_C_Q, _C_K) * (GLA_DK ** -0.5)
    gk_ref[...] = mm(_C_K, _C_V)
    gv_ref[...] = mm(_C_V, _C_R).astype(BF16)
    r = mm(_C_R, _C_G)
    gr_ref[...] = (r * _sigmoid(r)).astype(BF16)
    z = jnp.dot(mm(_C_G, _IN_COLS_PAD).astype(BF16), gw_ref[...],
                preferred_element_type=F32) + gb_ref[...]
    la_ref[...] = (jnp.minimum(z, 0.0) - jnp.log1p(jnp.exp(-jnp.abs(z)))) * (1.0 / GLA_TAU)


def _in_proj(x2, g, w, gw, gb, tm):
    t = x2.shape[0]
    row = lambda n: pl.BlockSpec((tm, n), lambda i: (i, 0))
    outs = [(CONV_CH, F32), (3 * MOBA_WIDTH, BF16), (GLA_HEADS * GLA_DK, F32), (GLA_HEADS * GLA_DK, F32),
            (GLA_WIDTH, BF16), (GLA_WIDTH, BF16), (GLA_HEADS * GLA_DK, F32)]
    return pl.pallas_call(
        _in_proj_body,
        grid=(t // tm,),
        in_specs=[row(D_MODEL), _resident((1, D_MODEL)), _resident(w.shape),
                  _resident(gw.shape), _resident(gb.shape)],
        out_specs=[row(n) for n, _ in outs],
        out_shape=[jax.ShapeDtypeStruct((t, n), d) for n, d in outs],
        compiler_params=_params(),
        name="in_proj",
    )(x2, g, w, gw, gb)


_CONV_PAD = 32
_CONV_ROWS = 64


def _conformer_body(glu_ref, w_ref, cb_ref, lg_ref, lb_ref, out_ref, hp_ref):
    seq = glu_ref.shape[0]
    halves = CONV_CH // LANES
    for c in range(halves):
        hp_ref[c, 0:_CONV_PAD, :] = jnp.zeros((_CONV_PAD, LANES), F32)
        hp_ref[c, _CONV_PAD:, :] = glu_ref[:, c * LANES:(c + 1) * LANES]

    def step(j, carry):
        r0 = pl.multiple_of(j * _CONV_ROWS, _CONV_ROWS)
        accs = []
        for c in range(halves):
            cols = slice(c * LANES, (c + 1) * LANES)
            acc = jnp.broadcast_to(cb_ref[:, cols], (_CONV_ROWS, LANES))
            for i in range(CONV_WIDTH):
                off = _CONV_PAD - (CONV_WIDTH - 1) + i
                acc = acc + w_ref[i:i + 1, cols] * hp_ref[c, pl.ds(r0 + off, _CONV_ROWS), :]
            accs.append(acc)
        y = jnp.concatenate(accs, axis=1)
        mu = jnp.mean(y, axis=-1, keepdims=True)
        d = y - mu
        var = jnp.mean(d * d, axis=-1, keepdims=True)
        y = d * lax.rsqrt(var + LN_EPS) * lg_ref[...] + lb_ref[...]
        out_ref[pl.ds(r0, _CONV_ROWS), :] = (y * _sigmoid(y)).astype(out_ref.dtype)
        return carry

    lax.fori_loop(0, seq // _CONV_ROWS, step, 0, unroll=4)


def _conformer(glu, w, cb, lg, lb, seq):
    t = glu.shape[0]
    return pl.pallas_call(
        _conformer_body,
        grid=(t // seq,),
        in_specs=[pl.BlockSpec((seq, CONV_CH), lambda b: (b, 0)), _resident(w.shape),
                  _resident(cb.shape), _resident(lg.shape), _resident(lb.shape)],
        out_specs=pl.BlockSpec((seq, CONV_CH), lambda b: (b, 0)),
        out_shape=jax.ShapeDtypeStruct((t, CONV_CH), BF16),
        scratch_shapes=[pltpu.VMEM((CONV_CH // LANES, _CONV_PAD + seq, LANES), F32)],
        compiler_params=_params(),
        name="conformer_conv",
    )(glu, w, cb, lg, lb)


_LOG2E = 1.4426950408889634


def _moba_body(q_ref, k_ref, v_ref, g_ref, out_ref, kaug_ref, qaug_t_ref, vaug_t_ref, s_a_ref, s_b_ref):
    seq = q_ref.shape[0]
    nb = seq // MOBA_BLOCK
    blk = MOBA_BLOCK
    dh = MOBA_DH
    aug = 2 * dh

    row_blk = lax.broadcasted_iota(jnp.int32, (seq, dh), 0) // blk
    lane = lax.broadcasted_iota(jnp.int32, (seq, dh), 1)
    onehot = jnp.where(lane == row_blk, 1.0, 0.0).astype(BF16)
    n_idx = lax.broadcasted_iota(jnp.int32, (nb, seq), 0)
    own = lax.broadcasted_iota(jnp.int32, (nb, seq), 1) // blk
    ones_row = jnp.where(lax.broadcasted_iota(jnp.int32, (SUBLANES, seq), 0) == 0, 1.0, 0.0)
    q_t = q_ref[...].astype(F32).T
    v_t = v_ref[...].astype(F32).T
    for h in range(MOBA_HEADS):
        hs = slice(h * dh, (h + 1) * dh)
        kh = k_ref[:, hs]
        kaug_ref[h] = jnp.concatenate([kh, onehot], axis=1)
        km = jnp.sum(kh.astype(F32).reshape(nb, blk, dh), axis=1) * (1.0 / blk)
        km_hi = km.astype(BF16)
        km_lo = (km - km_hi.astype(F32)).astype(BF16)
        qh_t = q_t[hs, :]
        qh_t16 = qh_t.astype(BF16)
        gate = (jnp.dot(km_hi, qh_t16, preferred_element_type=F32)
                + jnp.dot(km_lo, qh_t16, preferred_element_type=F32))
        gm = jnp.where(n_idx < own, gate, NEG_INF)
        rank = jnp.zeros((nb, seq), F32)
        for m in range(nb):
            gm_m = gm[m:m + 1, :]
            beats = (gm_m > gm) | ((gm_m == gm) & (m < n_idx))
            rank = rank + jnp.where(beats, 1.0, 0.0)
        keep = ((rank < MOBA_TOPK) & (n_idx < own)) | (n_idx >= own)
        bias = jnp.where(keep, 0.0, NEG_INF)
        qaug_t_ref[h] = jnp.concatenate(
            [qh_t * (dh ** -0.5 * _LOG2E), bias, jnp.zeros((aug - dh - nb, seq), F32)], axis=0).astype(BF16)
        vaug_t_ref[h] = jnp.concatenate(
            [v_t[hs, :], ones_row, jnp.zeros((aug - dh - SUBLANES, seq), F32)], axis=0).astype(BF16)

    causal = (lax.broadcasted_iota(jnp.int32, (blk, blk), 0)
              <= lax.broadcasted_iota(jnp.int32, (blk, blk), 1))

    def scores(i, h, s_ref):
        cur = slice(i * blk, (i + 1) * blk)
        qa = qaug_t_ref[h, :, cur]
        s_own = jnp.dot(kaug_ref[h, cur, :], qa, preferred_element_type=F32)
        s_own = jnp.where(causal, s_own, NEG_INF)
        s_ref[cur, :] = s_own
        m_col = jnp.max(s_own, axis=0, keepdims=True)
        if i > 0:
            s_past = jnp.dot(kaug_ref[h, 0:i * blk, :], qa, preferred_element_type=F32)
            s_ref[0:i * blk, :] = s_past
            m_col = jnp.maximum(m_col, jnp.max(s_past, axis=0, keepdims=True))
        return m_col

    def finish(i, h, s_ref, m_col):
        keys = slice(0, (i + 1) * blk)
        e = jnp.exp2(s_ref[keys, :] - m_col).astype(BF16)
        o = jnp.dot(vaug_t_ref[h, :, keys], e, preferred_element_type=F32)
        o = o[0:dh, :] / o[dh:dh + 1, :]
        return o * lax.rsqrt(jnp.mean(o * o, axis=0, keepdims=True) + NORM_EPS)

    units = [(i, h) for i in range(nb) for h in range(MOBA_HEADS)]
    bufs = (s_a_ref, s_b_ref)
    m_next = scores(*units[0], bufs[0])
    outs = []
    for u, (i, h) in enumerate(units):
        m_col = m_next
        if u + 1 < len(units):
            m_next = scores(*units[u + 1], bufs[(u + 1) % 2])
        outs.append(finish(i, h, bufs[u % 2], m_col))
        if h == MOBA_HEADS - 1:
            y = jnp.concatenate(outs, axis=0).T
            out_ref[i * blk:(i + 1) * blk, :] = (y * g_ref[...]).astype(out_ref.dtype)
            outs = []


def _moba(qkv, g, seq):
    t = qkv.shape[0]
    col = lambda c: pl.BlockSpec((seq, MOBA_WIDTH), lambda b: (b, c))
    return pl.pallas_call(
        _moba_body,
        grid=(t // seq,),
        in_specs=[col(0), col(1), col(2), _resident(g.shape)],
        out_specs=pl.BlockSpec((seq, MOBA_WIDTH), lambda b: (b, 0)),
        out_shape=jax.ShapeDtypeStruct((t, MOBA_WIDTH), BF16),
        scratch_shapes=[pltpu.VMEM((MOBA_HEADS, seq, 2 * MOBA_DH), BF16),
                        pltpu.VMEM((MOBA_HEADS, 2 * MOBA_DH, seq), BF16),
                        pltpu.VMEM((MOBA_HEADS, 2 * MOBA_DH, seq), BF16),
                        pltpu.VMEM((seq, MOBA_BLOCK), F32),
                        pltpu.VMEM((seq, MOBA_BLOCK), F32)],
        compiler_params=_params(),
        name="moba",
    )(qkv, qkv, qkv, g)


def _split3(x):
    hi = x.astype(BF16)
    r = x - hi.astype(F32)
    mid = r.astype(BF16)
    lo = (r - mid.astype(F32)).astype(BF16)
    return hi, mid, lo


def _gla_body(q_ref, k_ref, v_ref, la_ref, r_ref, g_ref, out_ref, st_ref):
    blk = GLA_BLOCK
    sub = GLA_SUB
    nsub = blk // sub
    dk = GLA_DK
    dv = GLA_DV

    @pl.when(pl.program_id(1) == 0)
    def _():
        st_ref[...] = jnp.zeros(st_ref.shape, F32)

    ri = lax.broadcasted_iota(jnp.int32, (blk, blk), 0)
    ci = lax.broadcasted_iota(jnp.int32, (blk, blk), 1)
    tril = jnp.where(ci <= ri, 1.0, 0.0).astype(BF16)
    causal = ci <= ri
    b_all = sum(jnp.dot(tril, part, preferred_element_type=F32) for part in _split3(la_ref[...]))

    def prep(h):
        ks = slice(h * dk, (h + 1) * dk)
        q = q_ref[:, ks]
        k = k_ref[:, ks]
        b = b_all[:, ks]
        s = [jnp.zeros((1, dk), F32)] + [b[I * sub - 1:I * sub, :] for I in range(1, nsub)]
        s_row = jnp.concatenate([jnp.broadcast_to(si, (sub, dk)) for si in s], axis=0)
        b_end = b[blk - 1:blk, :]
        qt = (q * jnp.exp(b - s_row)).astype(BF16)
        q_in = (q * jnp.exp(b)).astype(BF16)
        k_out = (k * jnp.exp(b_end - b)).astype(BF16)
        a_rows = []
        for I in range(nsub):
            n = (I + 1) * sub
            k_i = (k[0:n, :] * jnp.exp(s[I] - b[0:n, :])).astype(BF16)
            if n < blk:
                k_i = jnp.concatenate([k_i, jnp.zeros((blk - n, dk), BF16)], axis=0)
            a_rows.append(_nt(qt[I * sub:n, :], k_i))
        a = jnp.where(causal, jnp.concatenate(a_rows, axis=0), 0.0).astype(BF16)
        return a, q_in, k_out, jnp.exp(b_end)

    def finish(h, a, q_in, k_out, decay):
        vs = slice(h * dv, (h + 1) * dv)
        v = v_ref[:, vs]
        st = st_ref[h]
        o = jnp.dot(a, v, preferred_element_type=F32) + _nt(q_in, st.astype(BF16))
        v_t = v.astype(F32).T.astype(BF16)
        st_ref[h] = st * decay + jnp.dot(v_t, k_out, preferred_element_type=F32)
        o = o * lax.rsqrt(jnp.mean(o * o, axis=-1, keepdims=True) + NORM_EPS) * g_ref[:, vs]
        out_ref[:, vs] = (o * r_ref[:, vs].astype(F32)).astype(out_ref.dtype)

    nxt = prep(0)
    for h in range(GLA_HEADS):
        cur = nxt
        if h + 1 < GLA_HEADS:
            nxt = prep(h + 1)
        finish(h, *cur)


def _gla(gq, gk, gv, la, gr, g, seq):
    t = gq.shape[0]
    nblk = seq // GLA_BLOCK
    blk = lambda n: pl.BlockSpec((GLA_BLOCK, n), lambda b, i: (b * nblk + i, 0))
    kw = GLA_HEADS * GLA_DK
    return pl.pallas_call(
        _gla_body,
        grid=(t // seq, nblk),
        in_specs=[blk(kw), blk(kw), blk(GLA_WIDTH), blk(kw), blk(GLA_WIDTH), _resident(g.shape)],
        out_specs=blk(GLA_WIDTH),
        out_shape=jax.ShapeDtypeStruct((t, GLA_WIDTH), BF16),
        scratch_shapes=[pltpu.VMEM((GLA_HEADS, GLA_DV, GLA_DK), F32)],
        compiler_params=_params(2),
        name="gla",
    )(gq, gk, gv, la, gr, g)


def _ffn_body(x_ref, ya_ref, yb_ref, yc_ref, wo_ref, g_ref, wup_ref, cw_ref, cb_ref, wdn_ref,
              fg_ref, out_ref, h_ref, acc_ref, ubuf_a, ubuf_b, carry_ref, *, tiles_per_seq, final_norm):
    tm = x_ref.shape[0]
    nch = D_FF // FFN_CHUNK
    halves = FFN_CHUNK // LANES
    taps = FFN_CONV_WIDTH
    o_b = CONV_CH
    o_c = CONV_CH + MOBA_WIDTH

    @pl.when(pl.program_id(0) % tiles_per_seq == 0)
    def _():
        carry_ref[...] = jnp.zeros(carry_ref.shape, F32)

    y = jnp.dot(ya_ref[...], wo_ref[0:o_b, :], preferred_element_type=F32)
    y = y + jnp.dot(yb_ref[...], wo_ref[o_b:o_c, :], preferred_element_type=F32)
    y = y + jnp.dot(yc_ref[...], wo_ref[o_c:, :], preferred_element_type=F32)
    x1 = x_ref[...] + y
    out_ref[...] = x1
    h_ref[...] = _rms(x1, g_ref[...]).astype(BF16)
    acc_ref[...] = jnp.zeros(acc_ref.shape, F32)

    def up(j, ubuf):
        for s in range(2):
            lo = s * D_FF + j * FFN_CHUNK
            u = jnp.dot(h_ref[...], wup_ref[:, lo:lo + FFN_CHUNK], preferred_element_type=F32)
            for c in range(halves):
                ubuf[s, c, SUBLANES:, :] = u[:, c * LANES:(c + 1) * LANES]

    def gate_down(j, ubuf):
        ys = []
        for s in range(2):
            outs = []
            for c in range(halves):
                lo = s * D_FF + j * FFN_CHUNK + c * LANES
                cols = slice(lo, lo + LANES)
                hist = carry_ref.at[s * nch + j, :, c * LANES:(c + 1) * LANES]
                ubuf[s, c, 0:SUBLANES, :] = hist[...]
                y = cb_ref[:, cols]
                for d in range(taps):
                    y = y + cw_ref[taps - 1 - d:taps - d, cols] * ubuf[s, c, pl.ds(SUBLANES - d, tm), :]
                hist[...] = ubuf[s, c, tm:tm + SUBLANES, :]
                outs.append(y)
            ys.append(jnp.concatenate(outs, axis=1))
        val, gate = ys
        a = (val * _sigmoid(val) * gate).astype(BF16)
        acc_ref[...] += jnp.dot(a, wdn_ref[j * FFN_CHUNK:(j + 1) * FFN_CHUNK, :],
                                preferred_element_type=F32)

    assert nch % 2 == 1
    up(0, ubuf_a)
    for j in range(0, nch - 1, 2):
        up(j + 1, ubuf_b)
        gate_down(j, ubuf_a)
        up(j + 2, ubuf_a)
        gate_down(j + 1, ubuf_b)
    gate_down(nch - 1, ubuf_a)
    y = out_ref[...] + acc_ref[...]
    if final_norm:
        y = _rms(y, fg_ref[...])
    out_ref[...] = y


def _ffn(x2, ya, yb, yc, wo, g, wup, cw, cb, wdn, fg, tm, seq, final_norm):
    t = x2.shape[0]
    row = lambda n: pl.BlockSpec((tm, n), lambda i: (i, 0))
    nch2 = 2 * D_FF // FFN_CHUNK
    body = functools.partial(_ffn_body, tiles_per_seq=seq // tm, final_norm=final_norm)
    return pl.pallas_call(
        body,
        grid=(t // tm,),
        in_specs=[row(D_MODEL), row(CONV_CH), row(MOBA_WIDTH), row(GLA_WIDTH), _resident(wo.shape),
                  _resident(g.shape), _resident(wup.shape), _resident(cw.shape),
                  _resident(cb.shape), _resident(wdn.shape), _resident(fg.shape)],
        out_specs=row(D_MODEL),
        out_shape=jax.ShapeDtypeStruct((t, D_MODEL), F32),
        scratch_shapes=[pltpu.VMEM((tm, D_MODEL), BF16),
                        pltpu.VMEM((tm, D_MODEL), F32),
                        pltpu.VMEM((2, FFN_CHUNK // LANES, SUBLANES + tm, LANES), F32),
                        pltpu.VMEM((2, FFN_CHUNK // LANES, SUBLANES + tm, LANES), F32),
                        pltpu.VMEM((nch2, SUBLANES, FFN_CHUNK), F32)],
        compiler_params=_params(),
        name="ffn",
    )(x2, ya, yb, yc, wo, g, wup, cw, cb, wdn, fg)


def _pack_w_in(w):
    o_g = 2304
    o_r = o_g + GLA_GATE_RANK
    pad = jnp.zeros((D_MODEL, _IN_COLS_PAD - _C_G - GLA_GATE_RANK), w.dtype)
    return jnp.concatenate([w[:, :o_g], w[:, o_r:], w[:, o_g:o_r], pad], axis=1).astype(BF16)


def kernel(x, norm_mix_g, w_in, conv_w, conv_b, conv_ln_g, conv_ln_b, moba_out_g,
           gla_gate_w, gla_gate_b, gla_out_g, w_out, norm_ffn_g, ffn_w_up, ffn_conv_w,
           ffn_conv_b, ffn_w_down, final_g):
    bsz, seq, d = x.shape
    depth = w_in.shape[0]
    x2 = x.reshape(bsz * seq, d)
    tm_in, tm = 1024, 512
    row = lambda v: v.reshape(1, -1).astype(F32)
    for l in range(depth):
        w_in_p = _pack_w_in(w_in[l])
        gw = jnp.concatenate([gla_gate_w[l], jnp.zeros((LANES - GLA_GATE_RANK, GLA_HEADS * GLA_DK), F32)],
                             axis=0).astype(BF16)
        glu, qkv, gq, gk, gv, gr, la = _in_proj(x2, row(norm_mix_g[l]), w_in_p, gw, row(gla_gate_b[l]), tm_in)
        ya = _conformer(glu, conv_w[l], row(conv_b[l]), row(conv_ln_g[l]), row(conv_ln_b[l]), seq)
        yb = _moba(qkv, row(moba_out_g[l]), seq)
        yc = _gla(gq, gk, gv, la, gr, row(gla_out_g[l]), seq)
        x2 = _ffn(x2, ya, yb, yc, w_out[l].astype(BF16), row(norm_ffn_g[l]),
                  ffn_w_up[l].astype(BF16), ffn_conv_w[l], row(ffn_conv_b[l]),
                  ffn_w_down[l].astype(BF16), row(final_g), tm, seq,
                  final_norm=(l == depth - 1))
    return x2.reshape(bsz, seq, d)
```

```python
import functools

import jax
import jax.numpy as jnp
from jax import lax
from jax.experimental import pallas as pl
from jax.experimental.pallas import tpu as pltpu

F32 = jnp.float32
BF16 = jnp.bfloat16

D_MODEL = 1024
CONV_CH = 256
CONV_WIDTH = 31
MOBA_HEADS = 4
MOBA_DH = 64
MOBA_WIDTH = 256
MOBA_BLOCK = 256
MOBA_TOPK = 3
GLA_HEADS = 4
GLA_DK = 64
GLA_DV = 128
GLA_WIDTH = 512
GLA_GATE_RANK = 16
GLA_TAU = 16.0
GLA_BLOCK = 256
GLA_SUB = 32
D_FF = 2816
FFN_CHUNK = 256
FFN_CONV_WIDTH = 3
NORM_EPS = 1e-6
LN_EPS = 1e-5
NEG_INF = -1e30
LANES = 128
SUBLANES = 8
VMEM_LIMIT = 56 * 1024 * 1024

_C_A = 0
_C_B = 512
_C_Q = 1280
_C_K = 1536
_C_V = 1792
_C_R = 2304
_C_G = 2816
_IN_COLS_PAD = 2944


def _sigmoid(x):
    return 1.0 / (1.0 + jnp.exp(-x))


def _rms(x, g):
    return x * lax.rsqrt(jnp.mean(x * x, axis=-1, keepdims=True) + NORM_EPS) * g


def _nt(a, b):
    return lax.dot_general(a, b, (((1,), (1,)), ((), ())), preferred_element_type=F32)


def _params(n_axes=1):
    return pltpu.CompilerParams(dimension_semantics=("arbitrary",) * n_axes,
                                vmem_limit_bytes=VMEM_LIMIT)


def _resident(shape):
    return pl.BlockSpec(shape, lambda *_: (0,) * len(shape), pipeline_mode=pl.Buffered(1))


def _resident_layer(stacked, layer):
    tail = stacked.shape[1:]
    return pl.BlockSpec((None,) + tail, lambda *_: (layer,) + (0,) * len(tail),
                        pipeline_mode=pl.Buffered(1))


def _in_proj_body(x_ref, g_ref, w_ref, gw_ref, gb_ref,
                  glu_ref, qkv_ref, gq_ref, gk_ref, gv_ref, gr_ref, la_ref):
    h = _rms(x_ref[...], g_ref[...]).astype(BF16)

    def mm(lo, hi):
        return jnp.dot(h, w_ref[:, lo:hi], preferred_element_type=F32)

    a = mm(_C_A, _C_B)
    glu_ref[...] = a[:, :CONV_CH] * _sigmoid(a[:, CONV_CH:])
    qkv_ref[...] = mm(_C_B, _C_Q).astype(BF16)
    gq_ref[...] = mm(_C_Q, _C_K) * (GLA_DK ** -0.5)
    gk_ref[...] = mm(_C_K, _C_V)
    gv_ref[...] = mm(_C_V, _C_R).astype(BF16)
    r = mm(_C_R, _C_G)
    gr_ref[...] = (r * _sigmoid(r)).astype(BF16)
    z = jnp.dot(mm(_C_G, _IN_COLS_PAD).astype(BF16), gw_ref[...],
                preferred_element_type=F32) + gb_ref[...]
    la_ref[...] = (jnp.minimum(z, 0.0) - jnp.log1p(jnp.exp(-jnp.abs(z)))) * (1.0 / GLA_TAU)


def _in_proj(x2, g, w, layer, gw, gb, tm):
    t = x2.shape[0]
    row = lambda n: pl.BlockSpec((tm, n), lambda i: (i, 0))
    outs = [(CONV_CH, F32), (3 * MOBA_WIDTH, BF16), (GLA_HEADS * GLA_DK, F32), (GLA_HEADS * GLA_DK, F32),
            (GLA_WIDTH, BF16), (GLA_WIDTH, BF16), (GLA_HEADS * GLA_DK, F32)]
    return pl.pallas_call(
        _in_proj_body,
        grid=(t // tm,),
        in_specs=[row(D_MODEL), _resident((1, D_MODEL)), _resident_layer(w, layer),
                  _resident(gw.shape), _resident(gb.shape)],
        out_specs=[row(n) for n, _ in outs],
        out_shape=[jax.ShapeDtypeStruct((t, n), d) for n, d in outs],
        compiler_params=_params(),
        name="in_proj",
    )(x2, g, w, gw, gb)


_CONV_PAD = 32
_CONV_ROWS = 64


def _conformer_body(glu_ref, w_ref, cb_ref, lg_ref, lb_ref, out_ref, hp_ref):
    seq = glu_ref.shape[0]
    halves = CONV_CH // LANES
    for c in range(halves):
        hp_ref[c, 0:_CONV_PAD, :] = jnp.zeros((_CONV_PAD, LANES), F32)
        hp_ref[c, _CONV_PAD:, :] = glu_ref[:, c * LANES:(c + 1) * LANES]

    def step(j, carry):
        r0 = pl.multiple_of(j * _CONV_ROWS, _CONV_ROWS)
        accs = []
        for c in range(halves):
            cols = slice(c * LANES, (c + 1) * LANES)
            acc = jnp.broadcast_to(cb_ref[:, cols], (_CONV_ROWS, LANES))
            for i in range(CONV_WIDTH):
                off = _CONV_PAD - (CONV_WIDTH - 1) + i
                acc = acc + w_ref[i:i + 1, cols] * hp_ref[c, pl.ds(r0 + off, _CONV_ROWS), :]
            accs.append(acc)
        y = jnp.concatenate(accs, axis=1)
        mu = jnp.mean(y, axis=-1, keepdims=True)
        d = y - mu
        var = jnp.mean(d * d, axis=-1, keepdims=True)
        y = d * lax.rsqrt(var + LN_EPS) * lg_ref[...] + lb_ref[...]
        out_ref[pl.ds(r0, _CONV_ROWS), :] = (y * _sigmoid(y)).astype(out_ref.dtype)
        return carry

    lax.fori_loop(0, seq // _CONV_ROWS, step, 0, unroll=4)


def _conformer(glu, w, cb, lg, lb, seq):
    t = glu.shape[0]
    return pl.pallas_call(
        _conformer_body,
        grid=(t // seq,),
        in_specs=[pl.BlockSpec((seq, CONV_CH), lambda b: (b, 0)), _resident(w.shape),
                  _resident(cb.shape), _resident(lg.shape), _resident(lb.shape)],
        out_specs=pl.BlockSpec((seq, CONV_CH), lambda b: (b, 0)),
        out_shape=jax.ShapeDtypeStruct((t, CONV_CH), BF16),
        scratch_shapes=[pltpu.VMEM((CONV_CH // LANES, _CONV_PAD + seq, LANES), F32)],
        compiler_params=_params(),
        name="conformer_conv",
    )(glu, w, cb, lg, lb)


_LOG2E = 1.4426950408889634


def _moba_body(q_ref, k_ref, v_ref, g_ref, out_ref, kaug_ref, qaug_t_ref, vaug_t_ref,
               s_a_ref, s_b_ref, e_a_ref, e_b_ref):
    seq = q_ref.shape[0]
    nb = seq // MOBA_BLOCK
    blk = MOBA_BLOCK
    dh = MOBA_DH
    aug = 2 * dh

    row_blk = lax.broadcasted_iota(jnp.int32, (seq, dh), 0) // blk
    lane = lax.broadcasted_iota(jnp.int32, (seq, dh), 1)
    onehot = jnp.where(lane == row_blk, 1.0, 0.0).astype(BF16)
    n_idx = lax.broadcasted_iota(jnp.int32, (nb, seq), 0)
    own = lax.broadcasted_iota(jnp.int32, (nb, seq), 1) // blk
    ones_row = jnp.where(lax.broadcasted_iota(jnp.int32, (SUBLANES, seq), 0) == 0, 1.0, 0.0)
    q_t = q_ref[...].astype(F32).T
    v_t = v_ref[...].astype(F32).T
    for h in range(MOBA_HEADS):
        hs = slice(h * dh, (h + 1) * dh)
        kh = k_ref[:, hs]
        kaug_ref[h] = jnp.concatenate([kh, onehot], axis=1)
        km = jnp.sum(kh.astype(F32).reshape(nb, blk, dh), axis=1) * (1.0 / blk)
        km_hi = km.astype(BF16)
        km_lo = (km - km_hi.astype(F32)).astype(BF16)
        qh_t = q_t[hs, :]
        qh_t16 = qh_t.astype(BF16)
        gate = (jnp.dot(km_hi, qh_t16, preferred_element_type=F32)
                + jnp.dot(km_lo, qh_t16, preferred_element_type=F32))
        gm = jnp.where(n_idx < own, gate, NEG_INF)
        rank = jnp.zeros((nb, seq), F32)
        for m in range(nb):
            gm_m = gm[m:m + 1, :]
            beats = (gm_m > gm) | ((gm_m == gm) & (m < n_idx))
            rank = rank + jnp.where(beats, 1.0, 0.0)
        keep = ((rank < MOBA_TOPK) & (n_idx < own)) | (n_idx >= own)
        bias = jnp.where(keep, 0.0, NEG_INF)
        qaug_t_ref[h] = jnp.concatenate(
            [qh_t * (dh ** -0.5 * _LOG2E), bias, jnp.zeros((aug - dh - nb, seq), F32)], axis=0).astype(BF16)
        vaug_t_ref[h] = jnp.concatenate(
            [v_t[hs, :], ones_row, jnp.zeros((aug - dh - SUBLANES, seq), F32)], axis=0).astype(BF16)

    causal = (lax.broadcasted_iota(jnp.int32, (blk, blk), 0)
              <= lax.broadcasted_iota(jnp.int32, (blk, blk), 1))

    def scores(i, h, s_ref):
        cur = slice(i * blk, (i + 1) * blk)
        qa = qaug_t_ref[h, :, cur]
        s_own = jnp.dot(kaug_ref[h, cur, :], qa, preferred_element_type=F32)
        s_own = jnp.where(causal, s_own, NEG_INF)
        s_ref[cur, :] = s_own
        m_col = jnp.max(s_own, axis=0, keepdims=True)
        if i > 0:
            s_past = jnp.dot(kaug_ref[h, 0:i * blk, :], qa, preferred_element_type=F32)
            s_ref[0:i * blk, :] = s_past
            m_col = jnp.maximum(m_col, jnp.max(s_past, axis=0, keepdims=True))
        return m_col

    def weights(i, s_ref, m_col, e_ref):
        keys = slice(0, (i + 1) * blk)
        e_ref[keys, :] = jnp.exp2(s_ref[keys, :] - m_col).astype(BF16)

    def finish(i, h, e_ref):
        keys = slice(0, (i + 1) * blk)
        o = jnp.dot(vaug_t_ref[h, :, keys], e_ref[keys, :], preferred_element_type=F32)
        o = o[0:dh, :] / o[dh:dh + 1, :]
        return o * lax.rsqrt(jnp.mean(o * o, axis=0, keepdims=True) + NORM_EPS)

    units = [(i, h) for i in range(nb) for h in range(MOBA_HEADS)]
    s_bufs = (s_a_ref, s_b_ref)
    e_bufs = (e_a_ref, e_b_ref)
    m_cols = {0: scores(*units[0], s_bufs[0]), 1: scores(*units[1], s_bufs[1])}
    weights(units[0][0], s_bufs[0], m_cols.pop(0), e_bufs[0])
    outs = []
    for u, (i, h) in enumerate(units):
        if u + 2 < len(units):
            m_cols[u + 2] = scores(*units[u + 2], s_bufs[u % 2])
        if u + 1 < len(units):
            weights(units[u + 1][0], s_bufs[(u + 1) % 2], m_cols.pop(u + 1), e_bufs[(u + 1) % 2])
        outs.append(finish(i, h, e_bufs[u % 2]))
        if h == MOBA_HEADS - 1:
            y = jnp.concatenate(outs, axis=0).T
            out_ref[i * blk:(i + 1) * blk, :] = (y * g_ref[...]).astype(out_ref.dtype)
            outs = []


def _moba(qkv, g, seq):
    t = qkv.shape[0]
    col = lambda c: pl.BlockSpec((seq, MOBA_WIDTH), lambda b: (b, c))
    return pl.pallas_call(
        _moba_body,
        grid=(t // seq,),
        in_specs=[col(0), col(1), col(2), _resident(g.shape)],
        out_specs=pl.BlockSpec((seq, MOBA_WIDTH), lambda b: (b, 0)),
        out_shape=jax.ShapeDtypeStruct((t, MOBA_WIDTH), BF16),
        scratch_shapes=[pltpu.VMEM((MOBA_HEADS, seq, 2 * MOBA_DH), BF16),
                        pltpu.VMEM((MOBA_HEADS, 2 * MOBA_DH, seq), BF16),
                        pltpu.VMEM((MOBA_HEADS, 2 * MOBA_DH, seq), BF16),
                        pltpu.VMEM((seq, MOBA_BLOCK), F32),
                        pltpu.VMEM((seq, MOBA_BLOCK), F32),
                        pltpu.VMEM((seq, MOBA_BLOCK), BF16),
                        pltpu.VMEM((seq, MOBA_BLOCK), BF16)],
        compiler_params=_params(),
        name="moba",
    )(qkv, qkv, qkv, g)


def _split3(x):
    hi = x.astype(BF16)
    r = x - hi.astype(F32)
    mid = r.astype(BF16)
    lo = (r - mid.astype(F32)).astype(BF16)
    return hi, mid, lo


def _gla_body(q_ref, k_ref, v_ref, la_ref, r_ref, g_ref, out_ref, st_ref):
    blk = GLA_BLOCK
    sub = GLA_SUB
    nsub = blk // sub
    dk = GLA_DK
    dv = GLA_DV

    @pl.when(pl.program_id(1) == 0)
    def _():
        st_ref[...] = jnp.zeros(st_ref.shape, F32)

    nblk = q_ref.shape[0] // blk
    ri = lax.broadcasted_iota(jnp.int32, (blk, blk), 0)
    ci = lax.broadcasted_iota(jnp.int32, (blk, blk), 1)
    tril = jnp.where(ci <= ri, 1.0, 0.0).astype(BF16)
    causal = ci <= ri
    b_all = [sum(jnp.dot(tril, part, preferred_element_type=F32)
                 for part in _split3(la_ref[n * blk:(n + 1) * blk, :])) for n in range(nblk)]

    pair = LANES // dk
    lane_head = lax.broadcasted_iota(jnp.int32, (1, LANES), 1) // dk

    def prep(n, p):
        ls = slice(p * LANES, (p + 1) * LANES)
        rows = slice(n * blk, (n + 1) * blk)
        q = q_ref[rows, ls]
        k = k_ref[rows, ls]
        b = b_all[n][:, ls]
        s = [jnp.zeros((1, LANES), F32)] + [b[I * sub - 1:I * sub, :] for I in range(1, nsub)]
        s_row = jnp.concatenate([jnp.broadcast_to(si, (sub, LANES)) for si in s], axis=0)
        b_end = b[blk - 1:blk, :]
        qt = (q * jnp.exp(b - s_row)).astype(BF16)
        q_in = (q * jnp.exp(b)).astype(BF16)
        k_out = (k * jnp.exp(b_end - b)).astype(BF16)
        k_sub = []
        for I in range(nsub):
            nk = (I + 1) * sub
            k_i = (k[0:nk, :] * jnp.exp(s[I] - b[0:nk, :])).astype(BF16)
            if nk < blk:
                k_i = jnp.concatenate([k_i, jnp.zeros((blk - nk, LANES), BF16)], axis=0)
            k_sub.append(k_i)
        heads = []
        for j in range(pair):
            mine = lane_head == j
            qt_h = jnp.where(mine, qt, 0.0)
            a = jnp.concatenate([_nt(qt_h[I * sub:(I + 1) * sub, :], k_sub[I]) for I in range(nsub)], axis=0)
            a = jnp.where(causal, a, 0.0).astype(BF16)
            heads.append((a, jnp.where(mine, q_in, 0.0), mine))
        return heads, k_out, jnp.exp(b_end)

    def finish(n, h, a, q_in, mine, k_out, decay):
        vs = slice(h * dv, (h + 1) * dv)
        rows = slice(n * blk, (n + 1) * blk)
        v = v_ref[rows, vs]
        st = st_ref[h]
        o = jnp.dot(a, v, preferred_element_type=F32) + _nt(q_in, st.astype(BF16))
        v_t = v.astype(F32).T.astype(BF16)
        st_ref[h] = jnp.where(mine, st * decay + jnp.dot(v_t, k_out, preferred_element_type=F32), 0.0)
        o = o * lax.rsqrt(jnp.mean(o * o, axis=-1, keepdims=True) + NORM_EPS) * g_ref[:, vs]
        out_ref[rows, vs] = (o * r_ref[rows, vs].astype(F32)).astype(out_ref.dtype)

    units = [(n, p) for n in range(nblk) for p in range(GLA_HEADS // pair)]
    nxt = prep(*units[0])
    for u, (n, p) in enumerate(units):
        heads, k_out, decay = nxt
        if u + 1 < len(units):
            nxt = prep(*units[u + 1])
        for j, (a, q_in, mine) in enumerate(heads):
            finish(n, p * pair + j, a, q_in, mine, k_out, decay)


_GLA_STEP_BLOCKS = 4


def _gla(gq, gk, gv, la, gr, g, seq):
    t = gq.shape[0]
    rows = GLA_BLOCK * _GLA_STEP_BLOCKS
    nblk = seq // rows
    blk = lambda n: pl.BlockSpec((rows, n), lambda b, i: (b * nblk + i, 0))
    kw = GLA_HEADS * GLA_DK
    return pl.pallas_call(
        _gla_body,
        grid=(t // seq, nblk),
        in_specs=[blk(kw), blk(kw), blk(GLA_WIDTH), blk(kw), blk(GLA_WIDTH), _resident(g.shape)],
        out_specs=blk(GLA_WIDTH),
        out_shape=jax.ShapeDtypeStruct((t, GLA_WIDTH), BF16),
        scratch_shapes=[pltpu.VMEM((GLA_HEADS, GLA_DV, LANES), F32)],
        compiler_params=_params(2),
        name="gla",
    )(gq, gk, gv, la, gr, g)


def _ffn_body(x_ref, ya_ref, yb_ref, yc_ref, wo_ref, g_ref, wup_ref, cw_ref, cb_ref, wdn_ref,
              fg_ref, out_ref, h_ref, acc_ref, ubuf_a, ubuf_b, carry_ref, *, tiles_per_seq, final_norm):
    tm = x_ref.shape[0]
    nch = D_FF // FFN_CHUNK
    halves = FFN_CHUNK // LANES
    taps = FFN_CONV_WIDTH
    o_b = CONV_CH
    o_c = CONV_CH + MOBA_WIDTH

    @pl.when(pl.program_id(0) % tiles_per_seq == 0)
    def _():
        carry_ref[...] = jnp.zeros(carry_ref.shape, F32)

    y = jnp.dot(ya_ref[...], wo_ref[0:o_b, :], preferred_element_type=F32)
    y = y + jnp.dot(yb_ref[...], wo_ref[o_b:o_c, :], preferred_element_type=F32)
    y = y + jnp.dot(yc_ref[...], wo_ref[o_c:, :], preferred_element_type=F32)
    x1 = x_ref[...] + y
    out_ref[...] = x1
    h_ref[...] = _rms(x1, g_ref[...]).astype(BF16)
    acc_ref[...] = jnp.zeros(acc_ref.shape, F32)

    def up(j, ubuf):
        for s in range(2):
            lo = s * D_FF + j * FFN_CHUNK
            u = jnp.dot(h_ref[...], wup_ref[:, lo:lo + FFN_CHUNK], preferred_element_type=F32)
            for c in range(halves):
                ubuf[s, c, SUBLANES:, :] = u[:, c * LANES:(c + 1) * LANES]

    def gate_down(j, ubuf):
        ys = []
        for s in range(2):
            outs = []
            for c in range(halves):
                lo = s * D_FF + j * FFN_CHUNK + c * LANES
                cols = slice(lo, lo + LANES)
                hist = carry_ref.at[s * nch + j, :, c * LANES:(c + 1) * LANES]
                ubuf[s, c, 0:SUBLANES, :] = hist[...]
                y = cb_ref[:, cols]
                for d in range(taps):
                    y = y + cw_ref[taps - 1 - d:taps - d, cols] * ubuf[s, c, pl.ds(SUBLANES - d, tm), :]
                hist[...] = ubuf[s, c, tm:tm + SUBLANES, :]
                outs.append(y)
            ys.append(jnp.concatenate(outs, axis=1))
        val, gate = ys
        a = (val * _sigmoid(val) * gate).astype(BF16)
        acc_ref[...] += jnp.dot(a, wdn_ref[j * FFN_CHUNK:(j + 1) * FFN_CHUNK, :],
                                preferred_element_type=F32)

    assert nch % 2 == 1
    up(0, ubuf_a)
    for j in range(0, nch - 1, 2):
        up(j + 1, ubuf_b)
        gate_down(j, ubuf_a)
        up(j + 2, ubuf_a)
        gate_down(j + 1, ubuf_b)
    gate_down(nch - 1, ubuf_a)
    y = out_ref[...] + acc_ref[...]
    if final_norm:
        y = _rms(y, fg_ref[...])
    out_ref[...] = y


def _ffn(x2, ya, yb, yc, wo, g, wup, cw, cb, wdn, fg, layer, tm, seq, final_norm):
    t = x2.shape[0]
    row = lambda n: pl.BlockSpec((tm, n), lambda i: (i, 0))
    nch2 = 2 * D_FF // FFN_CHUNK
    body = functools.partial(_ffn_body, tiles_per_seq=seq // tm, final_norm=final_norm)
    return pl.pallas_call(
        body,
        grid=(t // tm,),
        in_specs=[row(D_MODEL), row(CONV_CH), row(MOBA_WIDTH), row(GLA_WIDTH), _resident_layer(wo, layer),
                  _resident(g.shape), _resident_layer(wup, layer), _resident(cw.shape),
                  _resident(cb.shape), _resident_layer(wdn, layer), _resident(fg.shape)],
        out_specs=row(D_MODEL),
        out_shape=jax.ShapeDtypeStruct((t, D_MODEL), F32),
        scratch_shapes=[pltpu.VMEM((tm, D_MODEL), BF16),
                        pltpu.VMEM((tm, D_MODEL), F32),
                        pltpu.VMEM((2, FFN_CHUNK // LANES, SUBLANES + tm, LANES), F32),
                        pltpu.VMEM((2, FFN_CHUNK // LANES, SUBLANES + tm, LANES), F32),
                        pltpu.VMEM((nch2, SUBLANES, FFN_CHUNK), F32)],
        compiler_params=_params(),
        name="ffn",
    )(x2, ya, yb, yc, wo, g, wup, cw, cb, wdn, fg)


_PREP_STEPS = 8


def _prep_body(win_ref, wout_ref, wup_ref, wdn_ref, win_o, wout_o, wup_o, wdn_o):
    o_g = _C_R
    o_r = o_g + GLA_GATE_RANK
    rows = win_ref.shape[0]
    win_o[:, 0:o_g] = win_ref[:, 0:o_g].astype(BF16)
    win_o[:, _C_R:_C_G] = win_ref[:, o_r:].astype(BF16)
    win_o[:, _C_G:] = jnp.concatenate(
        [win_ref[:, o_g:o_r], jnp.zeros((rows, _IN_COLS_PAD - _C_G - GLA_GATE_RANK), F32)], axis=1).astype(BF16)
    wout_o[...] = wout_ref[...].astype(BF16)
    wup_o[...] = wup_ref[...].astype(BF16)
    wdn_o[...] = wdn_ref[...].astype(BF16)


def _prep_weights(w_in, w_out, w_up, w_dn):
    depth = w_in.shape[0]

    def slab(arr, cols=None):
        r, c = arr.shape[1] // _PREP_STEPS, (cols or arr.shape[2])
        return pl.BlockSpec((None, r, c), lambda l, i: (l, i, 0))

    outs = [(w_in.shape[1], _IN_COLS_PAD), w_out.shape[1:], w_up.shape[1:], w_dn.shape[1:]]
    return pl.pallas_call(
        _prep_body,
        grid=(depth, _PREP_STEPS),
        in_specs=[slab(w_in), slab(w_out), slab(w_up), slab(w_dn)],
        out_specs=[slab(w_in, _IN_COLS_PAD), slab(w_out), slab(w_up), slab(w_dn)],
        out_shape=[jax.ShapeDtypeStruct((depth,) + tuple(s), BF16) for s in outs],
        compiler_params=_params(2),
        name="weight_cast",
    )(w_in, w_out, w_up, w_dn)


def kernel(x, norm_mix_g, w_in, conv_w, conv_b, conv_ln_g, conv_ln_b, moba_out_g,
           gla_gate_w, gla_gate_b, gla_out_g, w_out, norm_ffn_g, ffn_w_up, ffn_conv_w,
           ffn_conv_b, ffn_w_down, final_g):
    bsz, seq, d = x.shape
    depth = w_in.shape[0]
    x2 = x.reshape(bsz * seq, d)
    tm_in, tm = 1024, 512
    row = lambda v: v.reshape(1, -1).astype(F32)
    w_in_p, w_out_p, w_up_p, w_dn_p = _prep_weights(w_in, w_out, ffn_w_up, ffn_w_down)
    for l in range(depth):
        gw = jnp.concatenate([gla_gate_w[l], jnp.zeros((LANES - GLA_GATE_RANK, GLA_HEADS * GLA_DK), F32)],
                             axis=0).astype(BF16)
        glu, qkv, gq, gk, gv, gr, la = _in_proj(x2, row(norm_mix_g[l]), w_in_p, l, gw,
                                                row(gla_gate_b[l]), tm_in)
        ya = _conformer(glu, conv_w[l], row(conv_b[l]), row(conv_ln_g[l]), row(conv_ln_b[l]), seq)
        yb = _moba(qkv, row(moba_out_g[l]), seq)
        yc = _gla(gq, gk, gv, la, gr, row(gla_out_g[l]), seq)
        x2 = _ffn(x2, ya, yb, yc, w_out_p, row(norm_ffn_g[l]), w_up_p, ffn_conv_w[l],
                  row(ffn_conv_b[l]), w_dn_p, row(final_g), l, tm, seq,
                  final_norm=(l == depth - 1))
    return x2.reshape(bsz, seq, d)
```

```python
import functools

import jax
import jax.numpy as jnp
from jax import lax
from jax.experimental import pallas as pl
from jax.experimental.pallas import tpu as pltpu

F32 = jnp.float32
BF16 = jnp.bfloat16

D_MODEL = 1024
CONV_CH = 256
CONV_WIDTH = 31
MOBA_HEADS = 4
MOBA_DH = 64
MOBA_WIDTH = 256
MOBA_BLOCK = 256
MOBA_TOPK = 3
GLA_HEADS = 4
GLA_DK = 64
GLA_DV = 128
GLA_WIDTH = 512
GLA_GATE_RANK = 16
GLA_TAU = 16.0
GLA_BLOCK = 256
GLA_SUB = 32
D_FF = 2816
FFN_CHUNK = 256
FFN_CONV_WIDTH = 3
NORM_EPS = 1e-6
LN_EPS = 1e-5
NEG_INF = -1e30
LANES = 128
SUBLANES = 8
VMEM_LIMIT = 56 * 1024 * 1024

_C_A = 0
_C_B = 512
_C_Q = 1280
_C_K = 1536
_C_V = 1792
_C_R = 2304
_C_G = 2816
_IN_COLS_PAD = 2944


def _sigmoid(x):
    return 1.0 / (1.0 + jnp.exp(-x))


def _rms(x, g):
    return x * lax.rsqrt(jnp.mean(x * x, axis=-1, keepdims=True) + NORM_EPS) * g


def _nt(a, b):
    return lax.dot_general(a, b, (((1,), (1,)), ((), ())), preferred_element_type=F32)


def _params(n_axes=1):
    return pltpu.CompilerParams(dimension_semantics=("arbitrary",) * n_axes,
                                vmem_limit_bytes=VMEM_LIMIT)


def _resident(shape):
    return pl.BlockSpec(shape, lambda *_: (0,) * len(shape), pipeline_mode=pl.Buffered(1))


def _resident_layer(stacked, layer):
    tail = stacked.shape[1:]
    return pl.BlockSpec((None,) + tail, lambda *_: (layer,) + (0,) * len(tail),
                        pipeline_mode=pl.Buffered(1))


_IN_PROJ_SUB = 512
_CONV_PAD = 32
_CONV_ROWS = 64


def _in_proj_body(x_ref, g_ref, w_ref, gw_ref, gb_ref, cw_ref, cb_ref, lg_ref, lb_ref,
                  ya_ref, qkv_ref, gq_ref, gk_ref, gv_ref, gr_ref, la_ref, hp_ref, *, tiles_per_seq):
    tm = x_ref.shape[0]
    halves = CONV_CH // LANES
    first = pl.program_id(0) % tiles_per_seq == 0

    @pl.when(first)
    def _():
        hp_ref[:, 0:_CONV_PAD, :] = jnp.zeros((halves, _CONV_PAD, LANES), F32)

    @pl.when(jnp.logical_not(first))
    def _():
        hp_ref[:, 0:_CONV_PAD, :] = hp_ref[:, tm:tm + _CONV_PAD, :]

    def norm(rows):
        return _rms(x_ref[rows, :], g_ref[...]).astype(BF16)

    def conformer(r0):
        accs = []
        for c in range(halves):
            cols = slice(c * LANES, (c + 1) * LANES)
            acc = jnp.broadcast_to(cb_ref[:, cols], (_CONV_ROWS, LANES))
            for i in range(CONV_WIDTH):
                off = r0 + _CONV_PAD - (CONV_WIDTH - 1) + i
                acc = acc + cw_ref[i:i + 1, cols] * hp_ref[c, off:off + _CONV_ROWS, :]
            accs.append(acc)
        y = jnp.concatenate(accs, axis=1)
        d = y - jnp.mean(y, axis=-1, keepdims=True)
        var = jnp.mean(d * d, axis=-1, keepdims=True)
        y = d * lax.rsqrt(var + LN_EPS) * lg_ref[...] + lb_ref[...]
        ya_ref[r0:r0 + _CONV_ROWS, :] = (y * _sigmoid(y)).astype(ya_ref.dtype)

    def project(rows, h):
        def mm(lo, hi):
            return _nt(h, w_ref[lo:hi, :])

        a = mm(_C_A, _C_B)
        glu = a[:, :CONV_CH] * _sigmoid(a[:, CONV_CH:])
        for c in range(halves):
            hp_ref[c, _CONV_PAD + rows.start:_CONV_PAD + rows.stop, :] = glu[:, c * LANES:(c + 1) * LANES]
        def moba_qkv():
            qkv_ref[rows, :] = mm(_C_B, _C_Q).astype(BF16)

        def gla_q():
            gq_ref[rows, :] = mm(_C_Q, _C_K) * (GLA_DK ** -0.5)

        def gla_k():
            gk_ref[rows, :] = mm(_C_K, _C_V)

        def gla_v():
            gv_ref[rows, :] = mm(_C_V, _C_R).astype(BF16)

        def gla_r():
            r = mm(_C_R, _C_G)
            gr_ref[rows, :] = (r * _sigmoid(r)).astype(BF16)

        def gla_gate():
            z = jnp.dot(mm(_C_G, _IN_COLS_PAD).astype(BF16), gw_ref[...],
                        preferred_element_type=F32) + gb_ref[...]
            la_ref[rows, :] = (jnp.minimum(z, 0.0) - jnp.log1p(jnp.exp(-jnp.abs(z)))) * (1.0 / GLA_TAU)

        groups = [moba_qkv, gla_q, gla_k, gla_v, gla_r, gla_gate]
        passes = list(range(rows.start, rows.stop, _CONV_ROWS))
        per = -(-len(passes) // len(groups))
        for gi, group in enumerate(groups):
            group()
            for r0 in passes[gi * per:(gi + 1) * per]:
                conformer(r0)

    parts = [slice(r0, r0 + _IN_PROJ_SUB) for r0 in range(0, tm, _IN_PROJ_SUB)]
    h_next = norm(parts[0])
    for k, rows in enumerate(parts):
        h = h_next
        if k + 1 < len(parts):
            h_next = norm(parts[k + 1])
        project(rows, h)


def _in_proj(x2, g, w, layer, gw, gb, cw, cb, lg, lb, tm, seq):
    t = x2.shape[0]
    row = lambda n: pl.BlockSpec((tm, n), lambda i: (i, 0))
    outs = [(CONV_CH, BF16), (3 * MOBA_WIDTH, BF16), (GLA_HEADS * GLA_DK, F32), (GLA_HEADS * GLA_DK, F32),
            (GLA_WIDTH, BF16), (GLA_WIDTH, BF16), (GLA_HEADS * GLA_DK, F32)]
    return pl.pallas_call(
        functools.partial(_in_proj_body, tiles_per_seq=seq // tm),
        grid=(t // tm,),
        in_specs=[row(D_MODEL), _resident((1, D_MODEL)), _resident_layer(w, layer),
                  _resident(gw.shape), _resident(gb.shape), _resident(cw.shape), _resident(cb.shape),
                  _resident(lg.shape), _resident(lb.shape)],
        out_specs=[row(n) for n, _ in outs],
        out_shape=[jax.ShapeDtypeStruct((t, n), d) for n, d in outs],
        scratch_shapes=[pltpu.VMEM((CONV_CH // LANES, _CONV_PAD + tm, LANES), F32)],
        compiler_params=_params(),
        name="in_proj",
    )(x2, g, w, gw, gb, cw, cb, lg, lb)


_LOG2E = 1.4426950408889634


def _moba_body(q_ref, k_ref, v_ref, g_ref, out_ref, kaug_ref, qaug_t_ref, vaug_t_ref,
               s_a_ref, s_b_ref, e_a_ref, e_b_ref):
    seq = q_ref.shape[0]
    nb = seq // MOBA_BLOCK
    blk = MOBA_BLOCK
    dh = MOBA_DH
    aug = 2 * dh

    row_blk = lax.broadcasted_iota(jnp.int32, (seq, dh), 0) // blk
    lane = lax.broadcasted_iota(jnp.int32, (seq, dh), 1)
    onehot = jnp.where(lane == row_blk, 1.0, 0.0).astype(BF16)
    n_idx = lax.broadcasted_iota(jnp.int32, (nb, seq), 0)
    own = lax.broadcasted_iota(jnp.int32, (nb, seq), 1) // blk
    ones_row = jnp.where(lax.broadcasted_iota(jnp.int32, (SUBLANES, seq), 0) == 0, 1.0, 0.0)
    q_t = q_ref[...].astype(F32).T
    v_t = v_ref[...].astype(F32).T
    for h in range(MOBA_HEADS):
        hs = slice(h * dh, (h + 1) * dh)
        kh = k_ref[:, hs]
        kaug_ref[h] = jnp.concatenate([kh, onehot], axis=1)
        km = jnp.sum(kh.astype(F32).reshape(nb, blk, dh), axis=1) * (1.0 / blk)
        km_hi = km.astype(BF16)
        km_lo = (km - km_hi.astype(F32)).astype(BF16)
        qh_t = q_t[hs, :]
        qh_t16 = qh_t.astype(BF16)
        gate = (jnp.dot(km_hi, qh_t16, preferred_element_type=F32)
                + jnp.dot(km_lo, qh_t16, preferred_element_type=F32))
        gm = jnp.where(n_idx < own, gate, NEG_INF)
        rank = jnp.zeros((nb, seq), F32)
        for m in range(nb):
            gm_m = gm[m:m + 1, :]
            beats = (gm_m > gm) | ((gm_m == gm) & (m < n_idx))
            rank = rank + jnp.where(beats, 1.0, 0.0)
        keep = ((rank < MOBA_TOPK) & (n_idx < own)) | (n_idx >= own)
        bias = jnp.where(keep, 0.0, NEG_INF)
        qaug_t_ref[h] = jnp.concatenate(
            [qh_t * (dh ** -0.5 * _LOG2E), bias, jnp.zeros((aug - dh - nb, seq), F32)], axis=0).astype(BF16)
        vaug_t_ref[h] = jnp.concatenate(
            [v_t[hs, :], ones_row, jnp.zeros((aug - dh - SUBLANES, seq), F32)], axis=0).astype(BF16)

    causal = (lax.broadcasted_iota(jnp.int32, (blk, blk), 0)
              <= lax.broadcasted_iota(jnp.int32, (blk, blk), 1))

    def scores(i, h, s_ref):
        cur = slice(i * blk, (i + 1) * blk)
        qa = qaug_t_ref[h, :, cur]
        s_own = jnp.dot(kaug_ref[h, cur, :], qa, preferred_element_type=F32)
        s_own = jnp.where(causal, s_own, NEG_INF)
        s_ref[cur, :] = s_own
        m_col = jnp.max(s_own, axis=0, keepdims=True)
        if i > 0:
            s_past = jnp.dot(kaug_ref[h, 0:i * blk, :], qa, preferred_element_type=F32)
            s_ref[0:i * blk, :] = s_past
            m_col = jnp.maximum(m_col, jnp.max(s_past, axis=0, keepdims=True))
        return m_col

    def weights(i, s_ref, m_col, e_ref):
        keys = slice(0, (i + 1) * blk)
        e_ref[keys, :] = jnp.exp2(s_ref[keys, :] - m_col).astype(BF16)

    def finish(i, h, e_ref):
        keys = slice(0, (i + 1) * blk)
        o = jnp.dot(vaug_t_ref[h, :, keys], e_ref[keys, :], preferred_element_type=F32)
        o = o[0:dh, :] / o[dh:dh + 1, :]
        return o * lax.rsqrt(jnp.mean(o * o, axis=0, keepdims=True) + NORM_EPS)

    units = [(i, h) for i in range(nb) for h in range(MOBA_HEADS)]
    s_bufs = (s_a_ref, s_b_ref)
    e_bufs = (e_a_ref, e_b_ref)
    m_cols = {0: scores(*units[0], s_bufs[0]), 1: scores(*units[1], s_bufs[1])}
    weights(units[0][0], s_bufs[0], m_cols.pop(0), e_bufs[0])
    outs = []
    for u, (i, h) in enumerate(units):
        if u + 2 < len(units):
            m_cols[u + 2] = scores(*units[u + 2], s_bufs[u % 2])
        if u + 1 < len(units):
            weights(units[u + 1][0], s_bufs[(u + 1) % 2], m_cols.pop(u + 1), e_bufs[(u + 1) % 2])
        outs.append(finish(i, h, e_bufs[u % 2]))
        if h == MOBA_HEADS - 1:
            y = jnp.concatenate(outs, axis=0).T
            out_ref[i * blk:(i + 1) * blk, :] = (y * g_ref[...]).astype(out_ref.dtype)
            outs = []


def _moba(qkv, g, seq):
    t = qkv.shape[0]
    col = lambda c: pl.BlockSpec((seq, MOBA_WIDTH), lambda b: (b, c))
    return pl.pallas_call(
        _moba_body,
        grid=(t // seq,),
        in_specs=[col(0), col(1), col(2), _resident(g.shape)],
        out_specs=pl.BlockSpec((seq, MOBA_WIDTH), lambda b: (b, 0)),
        out_shape=jax.ShapeDtypeStruct((t, MOBA_WIDTH), BF16),
        scratch_shapes=[pltpu.VMEM((MOBA_HEADS, seq, 2 * MOBA_DH), BF16),
                        pltpu.VMEM((MOBA_HEADS, 2 * MOBA_DH, seq), BF16),
                        pltpu.VMEM((MOBA_HEADS, 2 * MOBA_DH, seq), BF16),
                        pltpu.VMEM((seq, MOBA_BLOCK), F32),
                        pltpu.VMEM((seq, MOBA_BLOCK), F32),
                        pltpu.VMEM((seq, MOBA_BLOCK), BF16),
                        pltpu.VMEM((seq, MOBA_BLOCK), BF16)],
        compiler_params=_params(),
        name="moba",
    )(qkv, qkv, qkv, g)


def _split3(x):
    hi = x.astype(BF16)
    r = x - hi.astype(F32)
    mid = r.astype(BF16)
    lo = (r - mid.astype(F32)).astype(BF16)
    return hi, mid, lo


def _gla_body(q_ref, k_ref, v_ref, la_ref, r_ref, g_ref, out_ref, st_ref):
    blk = GLA_BLOCK
    sub = GLA_SUB
    nsub = blk // sub
    dk = GLA_DK
    dv = GLA_DV

    @pl.when(pl.program_id(1) == 0)
    def _():
        st_ref[...] = jnp.zeros(st_ref.shape, F32)

    nblk = q_ref.shape[0] // blk
    ri = lax.broadcasted_iota(jnp.int32, (blk, blk), 0)
    ci = lax.broadcasted_iota(jnp.int32, (blk, blk), 1)
    tril = jnp.where(ci <= ri, 1.0, 0.0).astype(BF16)
    causal = ci <= ri
    b_all = [sum(jnp.dot(tril, part, preferred_element_type=F32)
                 for part in _split3(la_ref[n * blk:(n + 1) * blk, :])) for n in range(nblk)]

    pair = LANES // dk
    lane_head = lax.broadcasted_iota(jnp.int32, (1, LANES), 1) // dk

    def prep(n, p):
        ls = slice(p * LANES, (p + 1) * LANES)
        rows = slice(n * blk, (n + 1) * blk)
        q = q_ref[rows, ls]
        k = k_ref[rows, ls]
        b = b_all[n][:, ls]
        s = [jnp.zeros((1, LANES), F32)] + [b[I * sub - 1:I * sub, :] for I in range(1, nsub)]
        s_row = jnp.concatenate([jnp.broadcast_to(si, (sub, LANES)) for si in s], axis=0)
        b_end = b[blk - 1:blk, :]
        qt = (q * jnp.exp(b - s_row)).astype(BF16)
        q_in = (q * jnp.exp(b)).astype(BF16)
        k_out = (k * jnp.exp(b_end - b)).astype(BF16)
        k_sub = []
        for I in range(nsub):
            nk = (I + 1) * sub
            k_i = (k[0:nk, :] * jnp.exp(s[I] - b[0:nk, :])).astype(BF16)
            if nk < blk:
                k_i = jnp.concatenate([k_i, jnp.zeros((blk - nk, LANES), BF16)], axis=0)
            k_sub.append(k_i)
        heads = []
        for j in range(pair):
            mine = lane_head == j
            qt_h = jnp.where(mine, qt, 0.0)
            a = jnp.concatenate([_nt(qt_h[I * sub:(I + 1) * sub, :], k_sub[I]) for I in range(nsub)], axis=0)
            a = jnp.where(causal, a, 0.0).astype(BF16)
            heads.append((a, jnp.where(mine, q_in, 0.0), mine))
        return heads, k_out, jnp.exp(b_end)

    def finish(n, h, a, q_in, mine, k_out, decay):
        vs = slice(h * dv, (h + 1) * dv)
        rows = slice(n * blk, (n + 1) * blk)
        v = v_ref[rows, vs]
        st = st_ref[h]
        o = jnp.dot(a, v, preferred_element_type=F32) + _nt(q_in, st.astype(BF16))
        v_t = v.astype(F32).T.astype(BF16)
        st_ref[h] = jnp.where(mine, st * decay + jnp.dot(v_t, k_out, preferred_element_type=F32), 0.0)
        o = o * lax.rsqrt(jnp.mean(o * o, axis=-1, keepdims=True) + NORM_EPS) * g_ref[:, vs]
        out_ref[rows, vs] = (o * r_ref[rows, vs].astype(F32)).astype(out_ref.dtype)

    units = [(n, p) for n in range(nblk) for p in range(GLA_HEADS // pair)]
    nxt = prep(*units[0])
    for u, (n, p) in enumerate(units):
        heads, k_out, decay = nxt
        if u + 1 < len(units):
            nxt = prep(*units[u + 1])
        for j, (a, q_in, mine) in enumerate(heads):
            finish(n, p * pair + j, a, q_in, mine, k_out, decay)


_GLA_STEP_BLOCKS = 4


def _gla(gq, gk, gv, la, gr, g, seq):
    t = gq.shape[0]
    rows = GLA_BLOCK * _GLA_STEP_BLOCKS
    nblk = seq // rows
    blk = lambda n: pl.BlockSpec((rows, n), lambda b, i: (b * nblk + i, 0))
    kw = GLA_HEADS * GLA_DK
    return pl.pallas_call(
        _gla_body,
        grid=(t // seq, nblk),
        in_specs=[blk(kw), blk(kw), blk(GLA_WIDTH), blk(kw), blk(GLA_WIDTH), _resident(g.shape)],
        out_specs=blk(GLA_WIDTH),
        out_shape=jax.ShapeDtypeStruct((t, GLA_WIDTH), BF16),
        scratch_shapes=[pltpu.VMEM((GLA_HEADS, GLA_DV, LANES), F32)],
        compiler_params=_params(2),
        name="gla",
    )(gq, gk, gv, la, gr, g)


def _ffn_body(x_ref, ya_ref, yb_ref, yc_ref, wo_ref, g_ref, wup_ref, cw_ref, cb_ref, wdn_ref,
              fg_ref, out_ref, h_ref, acc_ref, ubuf_a, ubuf_b, carry_ref, *, tiles_per_seq, final_norm):
    tm = x_ref.shape[0]
    nch = D_FF // FFN_CHUNK
    halves = FFN_CHUNK // LANES
    taps = FFN_CONV_WIDTH
    o_b = CONV_CH
    o_c = CONV_CH + MOBA_WIDTH

    @pl.when(pl.program_id(0) % tiles_per_seq == 0)
    def _():
        carry_ref[...] = jnp.zeros(carry_ref.shape, F32)

    y = jnp.dot(ya_ref[...], wo_ref[0:o_b, :], preferred_element_type=F32)
    y = y + jnp.dot(yb_ref[...], wo_ref[o_b:o_c, :], preferred_element_type=F32)
    y = y + jnp.dot(yc_ref[...], wo_ref[o_c:, :], preferred_element_type=F32)
    x1 = x_ref[...] + y
    out_ref[...] = x1
    h_ref[...] = _rms(x1, g_ref[...]).astype(BF16)
    acc_ref[...] = jnp.zeros(acc_ref.shape, F32)

    def up(j, ubuf):
        for s in range(2):
            lo = s * D_FF + j * FFN_CHUNK
            u = jnp.dot(h_ref[...], wup_ref[:, lo:lo + FFN_CHUNK], preferred_element_type=F32)
            for c in range(halves):
                ubuf[s, c, SUBLANES:, :] = u[:, c * LANES:(c + 1) * LANES]

    def gate_down(j, ubuf):
        ys = []
        for s in range(2):
            outs = []
            for c in range(halves):
                lo = s * D_FF + j * FFN_CHUNK + c * LANES
                cols = slice(lo, lo + LANES)
                hist = carry_ref.at[s * nch + j, :, c * LANES:(c + 1) * LANES]
                ubuf[s, c, 0:SUBLANES, :] = hist[...]
                y = cb_ref[:, cols]
                for d in range(taps):
                    y = y + cw_ref[taps - 1 - d:taps - d, cols] * ubuf[s, c, pl.ds(SUBLANES - d, tm), :]
                hist[...] = ubuf[s, c, tm:tm + SUBLANES, :]
                outs.append(y)
            ys.append(jnp.concatenate(outs, axis=1))
        val, gate = ys
        a = (val * _sigmoid(val) * gate).astype(BF16)
        acc_ref[...] += jnp.dot(a, wdn_ref[j * FFN_CHUNK:(j + 1) * FFN_CHUNK, :],
                                preferred_element_type=F32)

    assert nch % 2 == 1
    up(0, ubuf_a)
    for j in range(0, nch - 1, 2):
        up(j + 1, ubuf_b)
        gate_down(j, ubuf_a)
        up(j + 2, ubuf_a)
        gate_down(j + 1, ubuf_b)
    gate_down(nch - 1, ubuf_a)
    y = out_ref[...] + acc_ref[...]
    if final_norm:
        y = _rms(y, fg_ref[...])
    out_ref[...] = y


def _ffn(x2, ya, yb, yc, wo, g, wup, cw, cb, wdn, fg, layer, tm, seq, final_norm):
    t = x2.shape[0]
    row = lambda n: pl.BlockSpec((tm, n), lambda i: (i, 0))
    nch2 = 2 * D_FF // FFN_CHUNK
    body = functools.partial(_ffn_body, tiles_per_seq=seq // tm, final_norm=final_norm)
    return pl.pallas_call(
        body,
        grid=(t // tm,),
        in_specs=[row(D_MODEL), row(CONV_CH), row(MOBA_WIDTH), row(GLA_WIDTH), _resident_layer(wo, layer),
                  _resident(g.shape), _resident_layer(wup, layer), _resident(cw.shape),
                  _resident(cb.shape), _resident_layer(wdn, layer), _resident(fg.shape)],
        out_specs=row(D_MODEL),
        out_shape=jax.ShapeDtypeStruct((t, D_MODEL), F32),
        scratch_shapes=[pltpu.VMEM((tm, D_MODEL), BF16),
                        pltpu.VMEM((tm, D_MODEL), F32),
                        pltpu.VMEM((2, FFN_CHUNK // LANES, SUBLANES + tm, LANES), F32),
                        pltpu.VMEM((2, FFN_CHUNK // LANES, SUBLANES + tm, LANES), F32),
                        pltpu.VMEM((nch2, SUBLANES, FFN_CHUNK), F32)],
        compiler_params=_params(),
        name="ffn",
    )(x2, ya, yb, yc, wo, g, wup, cw, cb, wdn, fg)


_PREP_STEPS = 8


def _prep_body(win_t_ref, wout_ref, wup_ref, wdn_ref, win_o, wout_o, wup_o, wdn_o):
    @pl.when(pl.program_id(1) == 0)
    def _():
        o_g = _C_R
        o_r = o_g + GLA_GATE_RANK
        win_o[0:o_g, :] = win_t_ref[0:o_g, :].astype(BF16)
        win_o[_C_R:_C_G, :] = win_t_ref[o_r:, :].astype(BF16)
        win_o[_C_G:_C_G + GLA_GATE_RANK, :] = win_t_ref[o_g:o_r, :].astype(BF16)
        win_o[_C_G + GLA_GATE_RANK:, :] = jnp.zeros(
            (_IN_COLS_PAD - _C_G - GLA_GATE_RANK, win_o.shape[1]), BF16)

    wout_o[...] = wout_ref[...].astype(BF16)
    wup_o[...] = wup_ref[...].astype(BF16)
    wdn_o[...] = wdn_ref[...].astype(BF16)


def _prep_weights(w_in_t, w_out, w_up, w_dn):
    depth, _, d = w_in_t.shape

    def slab(arr):
        return pl.BlockSpec((None, arr.shape[1] // _PREP_STEPS, arr.shape[2]), lambda l, i: (l, i, 0))

    whole = lambda rows: pl.BlockSpec((None, rows, d), lambda l, i: (l, 0, 0))
    outs = [(_IN_COLS_PAD, d), w_out.shape[1:], w_up.shape[1:], w_dn.shape[1:]]
    return pl.pallas_call(
        _prep_body,
        grid=(depth, _PREP_STEPS),
        in_specs=[whole(w_in_t.shape[1]), slab(w_out), slab(w_up), slab(w_dn)],
        out_specs=[whole(_IN_COLS_PAD), slab(w_out), slab(w_up), slab(w_dn)],
        out_shape=[jax.ShapeDtypeStruct((depth,) + tuple(s), BF16) for s in outs],
        compiler_params=_params(2),
        name="weight_cast",
    )(w_in_t, w_out, w_up, w_dn)


def kernel(x, norm_mix_g, w_in, conv_w, conv_b, conv_ln_g, conv_ln_b, moba_out_g,
           gla_gate_w, gla_gate_b, gla_out_g, w_out, norm_ffn_g, ffn_w_up, ffn_conv_w,
           ffn_conv_b, ffn_w_down, final_g):
    bsz, seq, d = x.shape
    depth = w_in.shape[0]
    x2 = x.reshape(bsz * seq, d)
    tm_in, tm = 1024, 512
    row = lambda v: v.reshape(1, -1).astype(F32)
    w_in_p, w_out_p, w_up_p, w_dn_p = _prep_weights(jnp.swapaxes(w_in, 1, 2), w_out, ffn_w_up, ffn_w_down)
    for l in range(depth):
        gw = jnp.concatenate([gla_gate_w[l], jnp.zeros((LANES - GLA_GATE_RANK, GLA_HEADS * GLA_DK), F32)],
                             axis=0).astype(BF16)
        ya, qkv, gq, gk, gv, gr, la = _in_proj(
            x2, row(norm_mix_g[l]), w_in_p, l, gw, row(gla_gate_b[l]), conv_w[l], row(conv_b[l]),
            row(conv_ln_g[l]), row(conv_ln_b[l]), tm_in, seq)
        yb = _moba(qkv, row(moba_out_g[l]), seq)
        yc = _gla(gq, gk, gv, la, gr, row(gla_out_g[l]), seq)
        x2 = _ffn(x2, ya, yb, yc, w_out_p, row(norm_ffn_g[l]), w_up_p, ffn_conv_w[l],
                  row(ffn_conv_b[l]), w_dn_p, row(final_g), l, tm, seq,
                  final_norm=(l == depth - 1))
    return x2.reshape(bsz, seq, d)
```

```python
import functools

import jax
import jax.numpy as jnp
from jax import lax
from jax.experimental import pallas as pl
from jax.experimental.pallas import tpu as pltpu

F32 = jnp.float32
BF16 = jnp.bfloat16

D_MODEL = 1024
CONV_CH = 256
CONV_WIDTH = 31
MOBA_HEADS = 4
MOBA_DH = 64
MOBA_WIDTH = 256
MOBA_BLOCK = 256
MOBA_TOPK = 3
GLA_HEADS = 4
GLA_DK = 64
GLA_DV = 128
GLA_WIDTH = 512
GLA_GATE_RANK = 16
GLA_TAU = 16.0
GLA_BLOCK = 256
GLA_SUB = 32
D_FF = 2816
FFN_CHUNK = 256
FFN_CONV_WIDTH = 3
NORM_EPS = 1e-6
LN_EPS = 1e-5
NEG_INF = -1e30
LANES = 128
SUBLANES = 8
VMEM_LIMIT = 56 * 1024 * 1024

_C_A = 0
_C_B = 512
_C_Q = 1280
_C_K = 1536
_C_V = 1792
_C_R = 2304
_C_G = 2816
_IN_COLS_PAD = 2944


def _sigmoid(x):
    return 1.0 / (1.0 + jnp.exp(-x))


def _rms(x, g):
    return x * lax.rsqrt(jnp.mean(x * x, axis=-1, keepdims=True) + NORM_EPS) * g


def _nt(a, b):
    return lax.dot_general(a, b, (((1,), (1,)), ((), ())), preferred_element_type=F32)


def _params(n_axes=1):
    return pltpu.CompilerParams(dimension_semantics=("arbitrary",) * n_axes,
                                vmem_limit_bytes=VMEM_LIMIT)


def _resident(shape):
    return pl.BlockSpec(shape, lambda *_: (0,) * len(shape), pipeline_mode=pl.Buffered(1))


def _resident_layer(stacked, layer):
    tail = stacked.shape[1:]
    return pl.BlockSpec((None,) + tail, lambda *_: (layer,) + (0,) * len(tail),
                        pipeline_mode=pl.Buffered(1))


_IN_PROJ_SUB = 512
_CONV_PAD = 32
_CONV_ROWS = 64


def _in_proj_body(x_ref, g_ref, w_ref, gw_ref, gb_ref, cw_ref, cb_ref, lg_ref, lb_ref,
                  ya_ref, qkv_ref, gq_ref, gk_ref, gv_ref, gr_ref, la_ref, hp_ref, *, tiles_per_seq):
    tm = x_ref.shape[0]
    halves = CONV_CH // LANES
    first = pl.program_id(0) % tiles_per_seq == 0

    @pl.when(first)
    def _():
        hp_ref[:, 0:_CONV_PAD, :] = jnp.zeros((halves, _CONV_PAD, LANES), F32)

    @pl.when(jnp.logical_not(first))
    def _():
        hp_ref[:, 0:_CONV_PAD, :] = hp_ref[:, tm:tm + _CONV_PAD, :]

    def norm(rows):
        return _rms(x_ref[rows, :], g_ref[...]).astype(BF16)

    def conformer(r0):
        accs = []
        for c in range(halves):
            cols = slice(c * LANES, (c + 1) * LANES)
            acc = jnp.broadcast_to(cb_ref[:, cols], (_CONV_ROWS, LANES))
            for i in range(CONV_WIDTH):
                off = r0 + _CONV_PAD - (CONV_WIDTH - 1) + i
                acc = acc + cw_ref[i:i + 1, cols] * hp_ref[c, off:off + _CONV_ROWS, :]
            accs.append(acc)
        y = jnp.concatenate(accs, axis=1)
        d = y - jnp.mean(y, axis=-1, keepdims=True)
        var = jnp.mean(d * d, axis=-1, keepdims=True)
        y = d * lax.rsqrt(var + LN_EPS) * lg_ref[...] + lb_ref[...]
        ya_ref[r0:r0 + _CONV_ROWS, :] = (y * _sigmoid(y)).astype(ya_ref.dtype)

    def project(rows, h):
        def mm(lo, hi):
            return _nt(h, w_ref[lo:hi, :])

        a = mm(_C_A, _C_B)
        glu = a[:, :CONV_CH] * _sigmoid(a[:, CONV_CH:])
        for c in range(halves):
            hp_ref[c, _CONV_PAD + rows.start:_CONV_PAD + rows.stop, :] = glu[:, c * LANES:(c + 1) * LANES]
        def moba_qkv():
            qkv_ref[rows, :] = mm(_C_B, _C_Q).astype(BF16)

        def gla_q():
            gq_ref[rows, :] = mm(_C_Q, _C_K) * (GLA_DK ** -0.5)

        def gla_k():
            gk_ref[rows, :] = mm(_C_K, _C_V)

        def gla_v():
            gv_ref[rows, :] = mm(_C_V, _C_R).astype(BF16)

        def gla_r():
            r = mm(_C_R, _C_G)
            gr_ref[rows, :] = (r * _sigmoid(r)).astype(BF16)

        def gla_gate():
            z = jnp.dot(mm(_C_G, _IN_COLS_PAD).astype(BF16), gw_ref[...],
                        preferred_element_type=F32) + gb_ref[...]
            la_ref[rows, :] = (jnp.minimum(z, 0.0) - jnp.log1p(jnp.exp(-jnp.abs(z)))) * (1.0 / GLA_TAU)

        groups = [moba_qkv, gla_q, gla_k, gla_v, gla_r, gla_gate]
        passes = list(range(rows.start, rows.stop, _CONV_ROWS))
        per = -(-len(passes) // len(groups))
        for gi, group in enumerate(groups):
            group()
            for r0 in passes[gi * per:(gi + 1) * per]:
                conformer(r0)

    parts = [slice(r0, r0 + _IN_PROJ_SUB) for r0 in range(0, tm, _IN_PROJ_SUB)]
    h_next = norm(parts[0])
    for k, rows in enumerate(parts):
        h = h_next
        if k + 1 < len(parts):
            h_next = norm(parts[k + 1])
        project(rows, h)


def _in_proj(x2, g, w, layer, gw, gb, cw, cb, lg, lb, tm, seq):
    t = x2.shape[0]
    row = lambda n: pl.BlockSpec((tm, n), lambda i: (i, 0))
    outs = [(CONV_CH, BF16), (3 * MOBA_WIDTH, BF16), (GLA_HEADS * GLA_DK, F32), (GLA_HEADS * GLA_DK, F32),
            (GLA_WIDTH, BF16), (GLA_WIDTH, BF16), (GLA_HEADS * GLA_DK, F32)]
    return pl.pallas_call(
        functools.partial(_in_proj_body, tiles_per_seq=seq // tm),
        grid=(t // tm,),
        in_specs=[row(D_MODEL), _resident((1, D_MODEL)), _resident_layer(w, layer),
                  _resident(gw.shape), _resident(gb.shape), _resident(cw.shape), _resident(cb.shape),
                  _resident(lg.shape), _resident(lb.shape)],
        out_specs=[row(n) for n, _ in outs],
        out_shape=[jax.ShapeDtypeStruct((t, n), d) for n, d in outs],
        scratch_shapes=[pltpu.VMEM((CONV_CH // LANES, _CONV_PAD + tm, LANES), F32)],
        compiler_params=_params(),
        name="in_proj",
    )(x2, g, w, gw, gb, cw, cb, lg, lb)


_LOG2E = 1.4426950408889634


def _moba_body(q_ref, k_ref, v_ref, g_ref, out_ref, kaug_ref, qaug_t_ref, vaug_t_ref,
               s_a_ref, s_b_ref, e_a_ref, e_b_ref):
    seq = q_ref.shape[0]
    nb = seq // MOBA_BLOCK
    blk = MOBA_BLOCK
    dh = MOBA_DH
    aug = 2 * dh

    row_blk = lax.broadcasted_iota(jnp.int32, (seq, dh), 0) // blk
    lane = lax.broadcasted_iota(jnp.int32, (seq, dh), 1)
    onehot = jnp.where(lane == row_blk, 1.0, 0.0).astype(BF16)
    n_idx = lax.broadcasted_iota(jnp.int32, (nb, seq), 0)
    own = lax.broadcasted_iota(jnp.int32, (nb, seq), 1) // blk
    ones_row = jnp.where(lax.broadcasted_iota(jnp.int32, (SUBLANES, seq), 0) == 0, 1.0, 0.0)
    q_t = q_ref[...].astype(F32).T
    v_t = v_ref[...].astype(F32).T
    for h in range(MOBA_HEADS):
        hs = slice(h * dh, (h + 1) * dh)
        kh = k_ref[:, hs]
        kaug_ref[h] = jnp.concatenate([kh, onehot], axis=1)
        km = jnp.sum(kh.astype(F32).reshape(nb, blk, dh), axis=1) * (1.0 / blk)
        km_hi = km.astype(BF16)
        km_lo = (km - km_hi.astype(F32)).astype(BF16)
        qh_t = q_t[hs, :]
        qh_t16 = qh_t.astype(BF16)
        gate = (jnp.dot(km_hi, qh_t16, preferred_element_type=F32)
                + jnp.dot(km_lo, qh_t16, preferred_element_type=F32))
        gm = jnp.where(n_idx < own, gate, NEG_INF)
        rank = jnp.zeros((nb, seq), F32)
        for m in range(nb):
            gm_m = gm[m:m + 1, :]
            beats = (gm_m > gm) | ((gm_m == gm) & (m < n_idx))
            rank = rank + jnp.where(beats, 1.0, 0.0)
        keep = ((rank < MOBA_TOPK) & (n_idx < own)) | (n_idx >= own)
        bias = jnp.where(keep, 0.0, NEG_INF)
        qaug_t_ref[h] = jnp.concatenate(
            [qh_t * (dh ** -0.5 * _LOG2E), bias, jnp.zeros((aug - dh - nb, seq), F32)], axis=0).astype(BF16)
        vaug_t_ref[h] = jnp.concatenate(
            [v_t[hs, :], ones_row, jnp.zeros((aug - dh - SUBLANES, seq), F32)], axis=0).astype(BF16)

    causal = (lax.broadcasted_iota(jnp.int32, (blk, blk), 0)
              <= lax.broadcasted_iota(jnp.int32, (blk, blk), 1))

    def scores(i, h, s_ref):
        cur = slice(i * blk, (i + 1) * blk)
        qa = qaug_t_ref[h, :, cur]
        s_own = jnp.dot(kaug_ref[h, cur, :], qa, preferred_element_type=F32)
        s_own = jnp.where(causal, s_own, NEG_INF)
        s_ref[cur, :] = s_own
        m_col = jnp.max(s_own, axis=0, keepdims=True)
        if i > 0:
            s_past = jnp.dot(kaug_ref[h, 0:i * blk, :], qa, preferred_element_type=F32)
            s_ref[0:i * blk, :] = s_past
            m_col = jnp.maximum(m_col, jnp.max(s_past, axis=0, keepdims=True))
        return m_col

    def weights(i, s_ref, m_col, e_ref):
        keys = slice(0, (i + 1) * blk)
        e_ref[keys, :] = jnp.exp2(s_ref[keys, :] - m_col).astype(BF16)

    def finish(i, h, e_ref):
        keys = slice(0, (i + 1) * blk)
        o = jnp.dot(vaug_t_ref[h, :, keys], e_ref[keys, :], preferred_element_type=F32)
        o = o[0:dh, :] / o[dh:dh + 1, :]
        return o * lax.rsqrt(jnp.mean(o * o, axis=0, keepdims=True) + NORM_EPS)

    units = [(i, h) for i in range(nb) for h in range(MOBA_HEADS)]
    s_bufs = (s_a_ref, s_b_ref)
    e_bufs = (e_a_ref, e_b_ref)
    m_cols = {0: scores(*units[0], s_bufs[0]), 1: scores(*units[1], s_bufs[1])}
    weights(units[0][0], s_bufs[0], m_cols.pop(0), e_bufs[0])
    outs = []
    for u, (i, h) in enumerate(units):
        if u + 2 < len(units):
            m_cols[u + 2] = scores(*units[u + 2], s_bufs[u % 2])
        if u + 1 < len(units):
            weights(units[u + 1][0], s_bufs[(u + 1) % 2], m_cols.pop(u + 1), e_bufs[(u + 1) % 2])
        outs.append(finish(i, h, e_bufs[u % 2]))
        if h == MOBA_HEADS - 1:
            y = jnp.concatenate(outs, axis=0).T
            out_ref[i * blk:(i + 1) * blk, :] = (y * g_ref[...]).astype(out_ref.dtype)
            outs = []


def _moba(qkv, g, seq):
    t = qkv.shape[0]
    col = lambda c: pl.BlockSpec((seq, MOBA_WIDTH), lambda b: (b, c))
    return pl.pallas_call(
        _moba_body,
        grid=(t // seq,),
        in_specs=[col(0), col(1), col(2), _resident(g.shape)],
        out_specs=pl.BlockSpec((seq, MOBA_WIDTH), lambda b: (b, 0)),
        out_shape=jax.ShapeDtypeStruct((t, MOBA_WIDTH), BF16),
        scratch_shapes=[pltpu.VMEM((MOBA_HEADS, seq, 2 * MOBA_DH), BF16),
                        pltpu.VMEM((MOBA_HEADS, 2 * MOBA_DH, seq), BF16),
                        pltpu.VMEM((MOBA_HEADS, 2 * MOBA_DH, seq), BF16),
                        pltpu.VMEM((seq, MOBA_BLOCK), F32),
                        pltpu.VMEM((seq, MOBA_BLOCK), F32),
                        pltpu.VMEM((seq, MOBA_BLOCK), BF16),
                        pltpu.VMEM((seq, MOBA_BLOCK), BF16)],
        compiler_params=_params(),
        name="moba",
    )(qkv, qkv, qkv, g)


def _split3(x):
    hi = x.astype(BF16)
    r = x - hi.astype(F32)
    mid = r.astype(BF16)
    lo = (r - mid.astype(F32)).astype(BF16)
    return hi, mid, lo


def _gla_body(q_ref, k_ref, v_ref, la_ref, r_ref, g_ref, out_ref, st_ref):
    blk = GLA_BLOCK
    sub = GLA_SUB
    nsub = blk // sub
    dk = GLA_DK
    dv = GLA_DV

    @pl.when(pl.program_id(1) == 0)
    def _():
        st_ref[...] = jnp.zeros(st_ref.shape, F32)

    nblk = q_ref.shape[0] // blk
    ri = lax.broadcasted_iota(jnp.int32, (blk, blk), 0)
    ci = lax.broadcasted_iota(jnp.int32, (blk, blk), 1)
    tril = jnp.where(ci <= ri, 1.0, 0.0).astype(BF16)
    causal = ci <= ri
    b_all = [sum(jnp.dot(tril, part, preferred_element_type=F32)
                 for part in _split3(la_ref[n * blk:(n + 1) * blk, :])) for n in range(nblk)]

    pair = LANES // dk
    lane_head = lax.broadcasted_iota(jnp.int32, (1, LANES), 1) // dk

    def prep(n, p):
        ls = slice(p * LANES, (p + 1) * LANES)
        rows = slice(n * blk, (n + 1) * blk)
        q = q_ref[rows, ls]
        k = k_ref[rows, ls]
        b = b_all[n][:, ls]
        s = [jnp.zeros((1, LANES), F32)] + [b[I * sub - 1:I * sub, :] for I in range(1, nsub)]
        s_row = jnp.concatenate([jnp.broadcast_to(si, (sub, LANES)) for si in s], axis=0)
        b_end = b[blk - 1:blk, :]
        qt = (q * jnp.exp(b - s_row)).astype(BF16)
        q_in = (q * jnp.exp(b)).astype(BF16)
        k_out = (k * jnp.exp(b_end - b)).astype(BF16)
        k_sub = []
        for I in range(nsub):
            nk = (I + 1) * sub
            k_i = (k[0:nk, :] * jnp.exp(s[I] - b[0:nk, :])).astype(BF16)
            if nk < blk:
                k_i = jnp.concatenate([k_i, jnp.zeros((blk - nk, LANES), BF16)], axis=0)
            k_sub.append(k_i)
        heads = []
        for j in range(pair):
            mine = lane_head == j
            qt_h = jnp.where(mine, qt, 0.0)
            a = jnp.concatenate([_nt(qt_h[I * sub:(I + 1) * sub, :], k_sub[I]) for I in range(nsub)], axis=0)
            a = jnp.where(causal, a, 0.0).astype(BF16)
            heads.append((a, jnp.where(mine, q_in, 0.0), mine))
        return heads, k_out, jnp.exp(b_end)

    def finish(n, h, a, q_in, mine, k_out, decay):
        vs = slice(h * dv, (h + 1) * dv)
        rows = slice(n * blk, (n + 1) * blk)
        v = v_ref[rows, vs]
        st = st_ref[h]
        o = jnp.dot(a, v, preferred_element_type=F32) + _nt(q_in, st.astype(BF16))
        v_t = v.astype(F32).T.astype(BF16)
        st_ref[h] = jnp.where(mine, st * decay + jnp.dot(v_t, k_out, preferred_element_type=F32), 0.0)
        o = o * lax.rsqrt(jnp.mean(o * o, axis=-1, keepdims=True) + NORM_EPS) * g_ref[:, vs]
        out_ref[rows, vs] = (o * r_ref[rows, vs].astype(F32)).astype(out_ref.dtype)

    units = [(n, p) for n in range(nblk) for p in range(GLA_HEADS // pair)]
    nxt = prep(*units[0])
    for u, (n, p) in enumerate(units):
        heads, k_out, decay = nxt
        if u + 1 < len(units):
            nxt = prep(*units[u + 1])
        for j, (a, q_in, mine) in enumerate(heads):
            finish(n, p * pair + j, a, q_in, mine, k_out, decay)


_GLA_STEP_BLOCKS = 8


def _gla(gq, gk, gv, la, gr, g, seq):
    t = gq.shape[0]
    rows = GLA_BLOCK * _GLA_STEP_BLOCKS
    nblk = seq // rows
    blk = lambda n: pl.BlockSpec((rows, n), lambda b, i: (b * nblk + i, 0))
    kw = GLA_HEADS * GLA_DK
    return pl.pallas_call(
        _gla_body,
        grid=(t // seq, nblk),
        in_specs=[blk(kw), blk(kw), blk(GLA_WIDTH), blk(kw), blk(GLA_WIDTH), _resident(g.shape)],
        out_specs=blk(GLA_WIDTH),
        out_shape=jax.ShapeDtypeStruct((t, GLA_WIDTH), BF16),
        scratch_shapes=[pltpu.VMEM((GLA_HEADS, GLA_DV, LANES), F32)],
        compiler_params=_params(2),
        name="gla",
    )(gq, gk, gv, la, gr, g)


def _ffn_body(x_ref, ya_ref, yb_ref, yc_ref, wo_ref, g_ref, wup_ref, cw_ref, cb_ref, wdn_ref,
              fg_ref, out_ref, h_ref, acc_ref, ubuf_a, ubuf_b, carry_ref, abuf_ref,
              *, tiles_per_seq, final_norm):
    tm = x_ref.shape[0]
    nch = D_FF // FFN_CHUNK
    halves = FFN_CHUNK // LANES
    taps = FFN_CONV_WIDTH
    o_b = CONV_CH
    o_c = CONV_CH + MOBA_WIDTH

    @pl.when(pl.program_id(0) % tiles_per_seq == 0)
    def _():
        carry_ref[...] = jnp.zeros(carry_ref.shape, F32)

    y = jnp.dot(ya_ref[...], wo_ref[0:o_b, :], preferred_element_type=F32)
    y = y + jnp.dot(yb_ref[...], wo_ref[o_b:o_c, :], preferred_element_type=F32)
    y = y + jnp.dot(yc_ref[...], wo_ref[o_c:, :], preferred_element_type=F32)
    x1 = x_ref[...] + y
    out_ref[...] = x1
    h_ref[...] = _rms(x1, g_ref[...]).astype(BF16)
    acc_ref[...] = jnp.zeros(acc_ref.shape, F32)

    def up(j, ubuf):
        for s in range(2):
            lo = s * D_FF + j * FFN_CHUNK
            u = jnp.dot(h_ref[...], wup_ref[:, lo:lo + FFN_CHUNK], preferred_element_type=F32)
            for c in range(halves):
                ubuf[s, c, SUBLANES:, :] = u[:, c * LANES:(c + 1) * LANES]

    def gate_down(j, ubuf):
        ys = []
        for s in range(2):
            outs = []
            for c in range(halves):
                lo = s * D_FF + j * FFN_CHUNK + c * LANES
                cols = slice(lo, lo + LANES)
                hist = carry_ref.at[s * nch + j, :, c * LANES:(c + 1) * LANES]
                ubuf[s, c, 0:SUBLANES, :] = hist[...]
                y = cb_ref[:, cols]
                for d in range(taps):
                    y = y + cw_ref[taps - 1 - d:taps - d, cols] * ubuf[s, c, pl.ds(SUBLANES - d, tm), :]
                hist[...] = ubuf[s, c, tm:tm + SUBLANES, :]
                outs.append(y)
            ys.append(jnp.concatenate(outs, axis=1))
        val, gate = ys
        a = (val * _sigmoid(val) * gate).astype(BF16)
        if j % 2 == 0 and j + 1 < nch:
            abuf_ref[:, 0:FFN_CHUNK] = a
        elif j % 2 == 1:
            abuf_ref[:, FFN_CHUNK:] = a
            acc_ref[...] += jnp.dot(abuf_ref[...], wdn_ref[(j - 1) * FFN_CHUNK:(j + 1) * FFN_CHUNK, :],
                                    preferred_element_type=F32)
        else:
            acc_ref[...] += jnp.dot(a, wdn_ref[j * FFN_CHUNK:(j + 1) * FFN_CHUNK, :],
                                    preferred_element_type=F32)

    assert nch % 2 == 1
    up(0, ubuf_a)
    for j in range(0, nch - 1, 2):
        up(j + 1, ubuf_b)
        gate_down(j, ubuf_a)
        up(j + 2, ubuf_a)
        gate_down(j + 1, ubuf_b)
    gate_down(nch - 1, ubuf_a)
    y = out_ref[...] + acc_ref[...]
    if final_norm:
        y = _rms(y, fg_ref[...])
    out_ref[...] = y


def _ffn(x2, ya, yb, yc, wo, g, wup, cw, cb, wdn, fg, layer, tm, seq, final_norm):
    t = x2.shape[0]
    row = lambda n: pl.BlockSpec((tm, n), lambda i: (i, 0))
    nch2 = 2 * D_FF // FFN_CHUNK
    body = functools.partial(_ffn_body, tiles_per_seq=seq // tm, final_norm=final_norm)
    return pl.pallas_call(
        body,
        grid=(t // tm,),
        in_specs=[row(D_MODEL), row(CONV_CH), row(MOBA_WIDTH), row(GLA_WIDTH), _resident_layer(wo, layer),
                  _resident(g.shape), _resident_layer(wup, layer), _resident(cw.shape),
                  _resident(cb.shape), _resident_layer(wdn, layer), _resident(fg.shape)],
        out_specs=row(D_MODEL),
        out_shape=jax.ShapeDtypeStruct((t, D_MODEL), F32),
        scratch_shapes=[pltpu.VMEM((tm, D_MODEL), BF16),
                        pltpu.VMEM((tm, D_MODEL), F32),
                        pltpu.VMEM((2, FFN_CHUNK // LANES, SUBLANES + tm, LANES), F32),
                        pltpu.VMEM((2, FFN_CHUNK // LANES, SUBLANES + tm, LANES), F32),
                        pltpu.VMEM((nch2, SUBLANES, FFN_CHUNK), F32),
                        pltpu.VMEM((tm, 2 * FFN_CHUNK), BF16)],
        compiler_params=_params(),
        name="ffn",
    )(x2, ya, yb, yc, wo, g, wup, cw, cb, wdn, fg)


_PREP_STEPS = 8


def _prep_body(win_t_ref, wout_ref, wup_ref, wdn_ref, win_o, wout_o, wup_o, wdn_o):
    @pl.when(pl.program_id(1) == 0)
    def _():
        o_g = _C_R
        o_r = o_g + GLA_GATE_RANK
        win_o[0:o_g, :] = win_t_ref[0:o_g, :].astype(BF16)
        win_o[_C_R:_C_G, :] = win_t_ref[o_r:, :].astype(BF16)
        win_o[_C_G:_C_G + GLA_GATE_RANK, :] = win_t_ref[o_g:o_r, :].astype(BF16)
        win_o[_C_G + GLA_GATE_RANK:, :] = jnp.zeros(
            (_IN_COLS_PAD - _C_G - GLA_GATE_RANK, win_o.shape[1]), BF16)

    wout_o[...] = wout_ref[...].astype(BF16)
    wup_o[...] = wup_ref[...].astype(BF16)
    wdn_o[...] = wdn_ref[...].astype(BF16)


def _prep_weights(w_in_t, w_out, w_up, w_dn):
    depth, _, d = w_in_t.shape

    def slab(arr):
        return pl.BlockSpec((None, arr.shape[1] // _PREP_STEPS, arr.shape[2]), lambda l, i: (l, i, 0))

    whole = lambda rows: pl.BlockSpec((None, rows, d), lambda l, i: (l, 0, 0))
    outs = [(_IN_COLS_PAD, d), w_out.shape[1:], w_up.shape[1:], w_dn.shape[1:]]
    return pl.pallas_call(
        _prep_body,
        grid=(depth, _PREP_STEPS),
        in_specs=[whole(w_in_t.shape[1]), slab(w_out), slab(w_up), slab(w_dn)],
        out_specs=[whole(_IN_COLS_PAD), slab(w_out), slab(w_up), slab(w_dn)],
        out_shape=[jax.ShapeDtypeStruct((depth,) + tuple(s), BF16) for s in outs],
        compiler_params=_params(2),
        name="weight_cast",
    )(w_in_t, w_out, w_up, w_dn)


def kernel(x, norm_mix_g, w_in, conv_w, conv_b, conv_ln_g, conv_ln_b, moba_out_g,
           gla_gate_w, gla_gate_b, gla_out_g, w_out, norm_ffn_g, ffn_w_up, ffn_conv_w,
           ffn_conv_b, ffn_w_down, final_g):
    bsz, seq, d = x.shape
    depth = w_in.shape[0]
    x2 = x.reshape(bsz * seq, d)
    tm_in, tm = 1024, 512
    row = lambda v: v.reshape(1, -1).astype(F32)
    w_in_p, w_out_p, w_up_p, w_dn_p = _prep_weights(jnp.swapaxes(w_in, 1, 2), w_out, ffn_w_up, ffn_w_down)
    for l in range(depth):
        gw = jnp.concatenate([gla_gate_w[l], jnp.zeros((LANES - GLA_GATE_RANK, GLA_HEADS * GLA_DK), F32)],
                             axis=0).astype(BF16)
        ya, qkv, gq, gk, gv, gr, la = _in_proj(
            x2, row(norm_mix_g[l]), w_in_p, l, gw, row(gla_gate_b[l]), conv_w[l], row(conv_b[l]),
            row(conv_ln_g[l]), row(conv_ln_b[l]), tm_in, seq)
        yb = _moba(qkv, row(moba_out_g[l]), seq)
        yc = _gla(gq, gk, gv, la, gr, row(gla_out_g[l]), seq)
        x2 = _ffn(x2, ya, yb, yc, w_out_p, row(norm_ffn_g[l]), w_up_p, ffn_conv_w[l],
                  row(ffn_conv_b[l]), w_dn_p, row(final_g), l, tm, seq,
                  final_norm=(l == depth - 1))
    return x2.reshape(bsz, seq, d)
```

```python
import functools

import jax
import jax.numpy as jnp
from jax import lax
from jax.experimental import pallas as pl
from jax.experimental.pallas import tpu as pltpu

F32 = jnp.float32
BF16 = jnp.bfloat16

D_MODEL = 1024
CONV_CH = 256
CONV_WIDTH = 31
MOBA_HEADS = 4
MOBA_DH = 64
MOBA_WIDTH = 256
MOBA_BLOCK = 256
MOBA_TOPK = 3
GLA_HEADS = 4
GLA_DK = 64
GLA_DV = 128
GLA_WIDTH = 512
GLA_GATE_RANK = 16
GLA_TAU = 16.0
GLA_BLOCK = 256
GLA_SUB = 32
D_FF = 2816
FFN_CHUNK = 256
FFN_CONV_WIDTH = 3
NORM_EPS = 1e-6
LN_EPS = 1e-5
NEG_INF = -1e30
LANES = 128
SUBLANES = 8
V7X_VMEM_BYTES = 64 * 1024 * 1024
VMEM_LIMIT = V7X_VMEM_BYTES - 8 * 1024 * 1024

_C_A = 0
_C_B = _C_A + 2 * CONV_CH
_C_Q = _C_B + 3 * MOBA_WIDTH
_C_K = _C_Q + GLA_HEADS * GLA_DK
_C_V = _C_K + GLA_HEADS * GLA_DK
_C_R = _C_V + GLA_WIDTH
_C_G = _C_R + GLA_WIDTH
_IN_COLS_PAD = _C_G + LANES


def _sigmoid(x):
    return 1.0 / (1.0 + jnp.exp(-x))


def _rms(x, g):
    return x * lax.rsqrt(jnp.mean(x * x, axis=-1, keepdims=True) + NORM_EPS) * g


def _nt(a, b):
    return lax.dot_general(a, b, (((1,), (1,)), ((), ())), preferred_element_type=F32)


def _params(n_axes=1):
    return pltpu.CompilerParams(dimension_semantics=("arbitrary",) * n_axes,
                                vmem_limit_bytes=VMEM_LIMIT)


def _resident(shape):
    return pl.BlockSpec(shape, lambda *_: (0,) * len(shape), pipeline_mode=pl.Buffered(1))


def _resident_layer(stacked, layer):
    tail = stacked.shape[1:]
    return pl.BlockSpec((None,) + tail, lambda *_: (layer,) + (0,) * len(tail),
                        pipeline_mode=pl.Buffered(1))


_IN_PROJ_TILE = 1024
_IN_PROJ_SUB = 512
_CONV_PAD = 32
_CONV_ROWS = 64


def _in_proj_body(x_ref, g_ref, w_ref, gw_ref, gb_ref, cw_ref, cb_ref, lg_ref, lb_ref,
                  ya_ref, qkv_ref, gq_ref, gk_ref, gv_ref, gr_ref, la_ref, hp_ref, *, tiles_per_seq):
    tm = x_ref.shape[0]
    halves = CONV_CH // LANES
    first = pl.program_id(0) % tiles_per_seq == 0

    @pl.when(first)
    def _():
        hp_ref[:, 0:_CONV_PAD, :] = jnp.zeros((halves, _CONV_PAD, LANES), F32)

    @pl.when(jnp.logical_not(first))
    def _():
        hp_ref[:, 0:_CONV_PAD, :] = hp_ref[:, tm:tm + _CONV_PAD, :]

    def norm(rows):
        return _rms(x_ref[rows, :], g_ref[...]).astype(BF16)

    def conformer(r0):
        accs = []
        for c in range(halves):
            cols = slice(c * LANES, (c + 1) * LANES)
            acc = jnp.broadcast_to(cb_ref[:, cols], (_CONV_ROWS, LANES))
            for i in range(CONV_WIDTH):
                off = r0 + _CONV_PAD - (CONV_WIDTH - 1) + i
                acc = acc + cw_ref[i:i + 1, cols] * hp_ref[c, off:off + _CONV_ROWS, :]
            accs.append(acc)
        y = jnp.concatenate(accs, axis=1)
        d = y - jnp.mean(y, axis=-1, keepdims=True)
        var = jnp.mean(d * d, axis=-1, keepdims=True)
        y = d * lax.rsqrt(var + LN_EPS) * lg_ref[...] + lb_ref[...]
        ya_ref[r0:r0 + _CONV_ROWS, :] = (y * _sigmoid(y)).astype(ya_ref.dtype)

    def project(rows, h):
        def mm(lo, hi):
            return _nt(h, w_ref[lo:hi, :])

        a = mm(_C_A, _C_B)
        glu = a[:, :CONV_CH] * _sigmoid(a[:, CONV_CH:])
        for c in range(halves):
            hp_ref[c, _CONV_PAD + rows.start:_CONV_PAD + rows.stop, :] = glu[:, c * LANES:(c + 1) * LANES]
        def moba_qkv():
            qkv_ref[rows, :] = mm(_C_B, _C_Q).astype(BF16)

        def gla_q():
            gq_ref[rows, :] = mm(_C_Q, _C_K) * (GLA_DK ** -0.5)

        def gla_k():
            gk_ref[rows, :] = mm(_C_K, _C_V)

        def gla_v():
            gv_ref[rows, :] = mm(_C_V, _C_R).astype(BF16)

        def gla_r():
            r = mm(_C_R, _C_G)
            gr_ref[rows, :] = (r * _sigmoid(r)).astype(BF16)

        def gla_gate():
            z = jnp.dot(mm(_C_G, _IN_COLS_PAD).astype(BF16), gw_ref[...],
                        preferred_element_type=F32) + gb_ref[...]
            la_ref[rows, :] = (jnp.minimum(z, 0.0) - jnp.log1p(jnp.exp(-jnp.abs(z)))) * (1.0 / GLA_TAU)

        groups = [moba_qkv, gla_q, gla_k, gla_v, gla_r, gla_gate]
        passes = list(range(rows.start, rows.stop, _CONV_ROWS))
        per = -(-len(passes) // len(groups))
        for gi, group in enumerate(groups):
            group()
            for r0 in passes[gi * per:(gi + 1) * per]:
                conformer(r0)

    parts = [slice(r0, r0 + _IN_PROJ_SUB) for r0 in range(0, tm, _IN_PROJ_SUB)]
    h_next = norm(parts[0])
    for k, rows in enumerate(parts):
        h = h_next
        if k + 1 < len(parts):
            h_next = norm(parts[k + 1])
        project(rows, h)


def _in_proj(x2, g, w, layer, gw, gb, cw, cb, lg, lb, tm, seq):
    t = x2.shape[0]
    row = lambda n: pl.BlockSpec((tm, n), lambda i: (i, 0))
    outs = [(CONV_CH, BF16), (3 * MOBA_WIDTH, BF16), (GLA_HEADS * GLA_DK, F32), (GLA_HEADS * GLA_DK, F32),
            (GLA_WIDTH, BF16), (GLA_WIDTH, BF16), (GLA_HEADS * GLA_DK, F32)]
    return pl.pallas_call(
        functools.partial(_in_proj_body, tiles_per_seq=seq // tm),
        grid=(t // tm,),
        in_specs=[row(D_MODEL), _resident((1, D_MODEL)), _resident_layer(w, layer),
                  _resident(gw.shape), _resident(gb.shape), _resident(cw.shape), _resident(cb.shape),
                  _resident(lg.shape), _resident(lb.shape)],
        out_specs=[row(n) for n, _ in outs],
        out_shape=[jax.ShapeDtypeStruct((t, n), d) for n, d in outs],
        scratch_shapes=[pltpu.VMEM((CONV_CH // LANES, _CONV_PAD + tm, LANES), F32)],
        compiler_params=_params(),
        name="in_proj",
    )(x2, g, w, gw, gb, cw, cb, lg, lb)


_LOG2E = 1.4426950408889634


def _moba_body(q_ref, k_ref, v_ref, g_ref, out_ref, kaug_ref, qaug_t_ref, vaug_t_ref,
               s_a_ref, s_b_ref, e_a_ref, e_b_ref):
    seq = q_ref.shape[0]
    nb = seq // MOBA_BLOCK
    blk = MOBA_BLOCK
    dh = MOBA_DH
    aug = 2 * dh

    row_blk = lax.broadcasted_iota(jnp.int32, (seq, dh), 0) // blk
    lane = lax.broadcasted_iota(jnp.int32, (seq, dh), 1)
    onehot = jnp.where(lane == row_blk, 1.0, 0.0).astype(BF16)
    n_idx = lax.broadcasted_iota(jnp.int32, (nb, seq), 0)
    own = lax.broadcasted_iota(jnp.int32, (nb, seq), 1) // blk
    ones_row = jnp.where(lax.broadcasted_iota(jnp.int32, (SUBLANES, seq), 0) == 0, 1.0, 0.0)
    q_t = q_ref[...].astype(F32).T
    v_t = v_ref[...].astype(F32).T
    for h in range(MOBA_HEADS):
        hs = slice(h * dh, (h + 1) * dh)
        kh = k_ref[:, hs]
        kaug_ref[h] = jnp.concatenate([kh, onehot], axis=1)
        km = jnp.sum(kh.astype(F32).reshape(nb, blk, dh), axis=1) * (1.0 / blk)
        km_hi = km.astype(BF16)
        km_lo = (km - km_hi.astype(F32)).astype(BF16)
        qh_t = q_t[hs, :]
        qh_t16 = qh_t.astype(BF16)
        gate = (jnp.dot(km_hi, qh_t16, preferred_element_type=F32)
                + jnp.dot(km_lo, qh_t16, preferred_element_type=F32))
        gm = jnp.where(n_idx < own, gate, NEG_INF)
        rank = jnp.zeros((nb, seq), F32)
        for m in range(nb):
            gm_m = gm[m:m + 1, :]
            beats = (gm_m > gm) | ((gm_m == gm) & (m < n_idx))
            rank = rank + jnp.where(beats, 1.0, 0.0)
        keep = ((rank < MOBA_TOPK) & (n_idx < own)) | (n_idx >= own)
        bias = jnp.where(keep, 0.0, NEG_INF)
        qaug_t_ref[h] = jnp.concatenate(
            [qh_t * (dh ** -0.5 * _LOG2E), bias, jnp.zeros((aug - dh - nb, seq), F32)], axis=0).astype(BF16)
        vaug_t_ref[h] = jnp.concatenate(
            [v_t[hs, :], ones_row, jnp.zeros((aug - dh - SUBLANES, seq), F32)], axis=0).astype(BF16)

    causal = (lax.broadcasted_iota(jnp.int32, (blk, blk), 0)
              <= lax.broadcasted_iota(jnp.int32, (blk, blk), 1))

    def scores(i, h, s_ref):
        cur = slice(i * blk, (i + 1) * blk)
        qa = qaug_t_ref[h, :, cur]
        s_own = jnp.dot(kaug_ref[h, cur, :], qa, preferred_element_type=F32)
        s_own = jnp.where(causal, s_own, NEG_INF)
        s_ref[cur, :] = s_own
        m_col = jnp.max(s_own, axis=0, keepdims=True)
        if i > 0:
            s_past = jnp.dot(kaug_ref[h, 0:i * blk, :], qa, preferred_element_type=F32)
            s_ref[0:i * blk, :] = s_past
            m_col = jnp.maximum(m_col, jnp.max(s_past, axis=0, keepdims=True))
        return m_col

    def weights(i, s_ref, m_col, e_ref):
        keys = slice(0, (i + 1) * blk)
        e_ref[keys, :] = jnp.exp2(s_ref[keys, :] - m_col).astype(BF16)

    def finish(i, h, e_ref):
        keys = slice(0, (i + 1) * blk)
        o = jnp.dot(vaug_t_ref[h, :, keys], e_ref[keys, :], preferred_element_type=F32)
        o = o[0:dh, :] / o[dh:dh + 1, :]
        return o * lax.rsqrt(jnp.mean(o * o, axis=0, keepdims=True) + NORM_EPS)

    units = [(i, h) for i in range(nb) for h in range(MOBA_HEADS)]
    s_bufs = (s_a_ref, s_b_ref)
    e_bufs = (e_a_ref, e_b_ref)
    m_cols = {0: scores(*units[0], s_bufs[0]), 1: scores(*units[1], s_bufs[1])}
    weights(units[0][0], s_bufs[0], m_cols.pop(0), e_bufs[0])
    outs = []
    for u, (i, h) in enumerate(units):
        if u + 2 < len(units):
            m_cols[u + 2] = scores(*units[u + 2], s_bufs[u % 2])
        if u + 1 < len(units):
            weights(units[u + 1][0], s_bufs[(u + 1) % 2], m_cols.pop(u + 1), e_bufs[(u + 1) % 2])
        outs.append(finish(i, h, e_bufs[u % 2]))
        if h == MOBA_HEADS - 1:
            y = jnp.concatenate(outs, axis=0).T
            out_ref[i * blk:(i + 1) * blk, :] = (y * g_ref[...]).astype(out_ref.dtype)
            outs = []


def _moba(qkv, g, seq):
    t = qkv.shape[0]
    col = lambda c: pl.BlockSpec((seq, MOBA_WIDTH), lambda b: (b, c))
    return pl.pallas_call(
        _moba_body,
        grid=(t // seq,),
        in_specs=[col(0), col(1), col(2), _resident(g.shape)],
        out_specs=pl.BlockSpec((seq, MOBA_WIDTH), lambda b: (b, 0)),
        out_shape=jax.ShapeDtypeStruct((t, MOBA_WIDTH), BF16),
        scratch_shapes=[pltpu.VMEM((MOBA_HEADS, seq, 2 * MOBA_DH), BF16),
                        pltpu.VMEM((MOBA_HEADS, 2 * MOBA_DH, seq), BF16),
                        pltpu.VMEM((MOBA_HEADS, 2 * MOBA_DH, seq), BF16),
                        pltpu.VMEM((seq, MOBA_BLOCK), F32),
                        pltpu.VMEM((seq, MOBA_BLOCK), F32),
                        pltpu.VMEM((seq, MOBA_BLOCK), BF16),
                        pltpu.VMEM((seq, MOBA_BLOCK), BF16)],
        compiler_params=_params(),
        name="moba",
    )(qkv, qkv, qkv, g)


def _split3(x):
    hi = x.astype(BF16)
    r = x - hi.astype(F32)
    mid = r.astype(BF16)
    lo = (r - mid.astype(F32)).astype(BF16)
    return hi, mid, lo


def _gla_body(q_ref, k_ref, v_ref, la_ref, r_ref, g_ref, out_ref, st_ref):
    blk = GLA_BLOCK
    sub = GLA_SUB
    nsub = blk // sub
    dk = GLA_DK
    dv = GLA_DV

    @pl.when(pl.program_id(1) == 0)
    def _():
        st_ref[...] = jnp.zeros(st_ref.shape, F32)

    nblk = q_ref.shape[0] // blk
    ri = lax.broadcasted_iota(jnp.int32, (blk, blk), 0)
    ci = lax.broadcasted_iota(jnp.int32, (blk, blk), 1)
    tril = jnp.where(ci <= ri, 1.0, 0.0).astype(BF16)
    causal = ci <= ri
    b_all = [sum(jnp.dot(tril, part, preferred_element_type=F32)
                 for part in _split3(la_ref[n * blk:(n + 1) * blk, :])) for n in range(nblk)]

    pair = LANES // dk
    lane_head = lax.broadcasted_iota(jnp.int32, (1, LANES), 1) // dk

    def prep(n, p):
        ls = slice(p * LANES, (p + 1) * LANES)
        rows = slice(n * blk, (n + 1) * blk)
        q = q_ref[rows, ls]
        k = k_ref[rows, ls]
        b = b_all[n][:, ls]
        s = [jnp.zeros((1, LANES), F32)] + [b[I * sub - 1:I * sub, :] for I in range(1, nsub)]
        s_row = jnp.concatenate([jnp.broadcast_to(si, (sub, LANES)) for si in s], axis=0)
        b_end = b[blk - 1:blk, :]
        qt = (q * jnp.exp(b - s_row)).astype(BF16)
        q_in = (q * jnp.exp(b)).astype(BF16)
        k_out = (k * jnp.exp(b_end - b)).astype(BF16)
        k_sub = []
        for I in range(nsub):
            nk = (I + 1) * sub
            k_i = (k[0:nk, :] * jnp.exp(s[I] - b[0:nk, :])).astype(BF16)
            if nk < blk:
                k_i = jnp.concatenate([k_i, jnp.zeros((blk - nk, LANES), BF16)], axis=0)
            k_sub.append(k_i)
        heads = []
        for j in range(pair):
            mine = lane_head == j
            qt_h = jnp.where(mine, qt, 0.0)
            a = jnp.concatenate([_nt(qt_h[I * sub:(I + 1) * sub, :], k_sub[I]) for I in range(nsub)], axis=0)
            a = jnp.where(causal, a, 0.0).astype(BF16)
            heads.append((a, jnp.where(mine, q_in, 0.0), mine))
        return heads, k_out, jnp.exp(b_end)

    def finish(n, h, a, q_in, mine, k_out, decay):
        vs = slice(h * dv, (h + 1) * dv)
        rows = slice(n * blk, (n + 1) * blk)
        v = v_ref[rows, vs]
        st = st_ref[h]
        o = jnp.dot(a, v, preferred_element_type=F32) + _nt(q_in, st.astype(BF16))
        v_t = v.astype(F32).T.astype(BF16)
        st_ref[h] = jnp.where(mine, st * decay + jnp.dot(v_t, k_out, preferred_element_type=F32), 0.0)
        o = o * lax.rsqrt(jnp.mean(o * o, axis=-1, keepdims=True) + NORM_EPS) * g_ref[:, vs]
        out_ref[rows, vs] = (o * r_ref[rows, vs].astype(F32)).astype(out_ref.dtype)

    units = [(n, p) for n in range(nblk) for p in range(GLA_HEADS // pair)]
    nxt = prep(*units[0])
    for u, (n, p) in enumerate(units):
        heads, k_out, decay = nxt
        if u + 1 < len(units):
            nxt = prep(*units[u + 1])
        for j, (a, q_in, mine) in enumerate(heads):
            finish(n, p * pair + j, a, q_in, mine, k_out, decay)


_GLA_STEP_BLOCKS = 4


def _gla(gq, gk, gv, la, gr, g, seq):
    t = gq.shape[0]
    rows = GLA_BLOCK * _GLA_STEP_BLOCKS
    nblk = seq // rows
    blk = lambda n: pl.BlockSpec((rows, n), lambda b, i: (b * nblk + i, 0))
    kw = GLA_HEADS * GLA_DK
    return pl.pallas_call(
        _gla_body,
        grid=(t // seq, nblk),
        in_specs=[blk(kw), blk(kw), blk(GLA_WIDTH), blk(kw), blk(GLA_WIDTH), _resident(g.shape)],
        out_specs=blk(GLA_WIDTH),
        out_shape=jax.ShapeDtypeStruct((t, GLA_WIDTH), BF16),
        scratch_shapes=[pltpu.VMEM((GLA_HEADS, GLA_DV, LANES), F32)],
        compiler_params=_params(2),
        name="gla",
    )(gq, gk, gv, la, gr, g)


_FFN_TILE = 512


def _ffn_body(x_ref, ya_ref, yb_ref, yc_ref, wo_ref, g_ref, wup_ref, cw_ref, cb_ref, wdn_ref,
              fg_ref, out_ref, h_ref, acc_ref, ubuf_a, ubuf_b, carry_ref, abuf_ref,
              *, tiles_per_seq, final_norm):
    tm = x_ref.shape[0]
    nch = D_FF // FFN_CHUNK
    halves = FFN_CHUNK // LANES
    taps = FFN_CONV_WIDTH
    o_b = CONV_CH
    o_c = CONV_CH + MOBA_WIDTH

    @pl.when(pl.program_id(0) % tiles_per_seq == 0)
    def _():
        carry_ref[...] = jnp.zeros(carry_ref.shape, F32)

    y = jnp.dot(ya_ref[...], wo_ref[0:o_b, :], preferred_element_type=F32)
    y = y + jnp.dot(yb_ref[...], wo_ref[o_b:o_c, :], preferred_element_type=F32)
    y = y + jnp.dot(yc_ref[...], wo_ref[o_c:, :], preferred_element_type=F32)
    x1 = x_ref[...] + y
    out_ref[...] = x1
    h_ref[...] = _rms(x1, g_ref[...]).astype(BF16)
    acc_ref[...] = jnp.zeros(acc_ref.shape, F32)

    def up(j, ubuf):
        for s in range(2):
            lo = s * D_FF + j * FFN_CHUNK
            u = jnp.dot(h_ref[...], wup_ref[:, lo:lo + FFN_CHUNK], preferred_element_type=F32)
            for c in range(halves):
                ubuf[s, c, SUBLANES:, :] = u[:, c * LANES:(c + 1) * LANES]

    def gate_down(j, ubuf):
        ys = []
        for s in range(2):
            outs = []
            for c in range(halves):
                lo = s * D_FF + j * FFN_CHUNK + c * LANES
                cols = slice(lo, lo + LANES)
                hist = carry_ref.at[s * nch + j, :, c * LANES:(c + 1) * LANES]
                ubuf[s, c, 0:SUBLANES, :] = hist[...]
                y = cb_ref[:, cols]
                for d in range(taps):
                    y = y + cw_ref[taps - 1 - d:taps - d, cols] * ubuf[s, c, pl.ds(SUBLANES - d, tm), :]
                hist[...] = ubuf[s, c, tm:tm + SUBLANES, :]
                outs.append(y)
            ys.append(jnp.concatenate(outs, axis=1))
        val, gate = ys
        a = (val * _sigmoid(val) * gate).astype(BF16)
        if j % 2 == 0 and j + 1 < nch:
            abuf_ref[:, 0:FFN_CHUNK] = a
        elif j % 2 == 1:
            abuf_ref[:, FFN_CHUNK:] = a
            acc_ref[...] += jnp.dot(abuf_ref[...], wdn_ref[(j - 1) * FFN_CHUNK:(j + 1) * FFN_CHUNK, :],
                                    preferred_element_type=F32)
        else:
            acc_ref[...] += jnp.dot(a, wdn_ref[j * FFN_CHUNK:(j + 1) * FFN_CHUNK, :],
                                    preferred_element_type=F32)

    assert nch % 2 == 1
    up(0, ubuf_a)
    for j in range(0, nch - 1, 2):
        up(j + 1, ubuf_b)
        gate_down(j, ubuf_a)
        up(j + 2, ubuf_a)
        gate_down(j + 1, ubuf_b)
    gate_down(nch - 1, ubuf_a)
    y = out_ref[...] + acc_ref[...]
    if final_norm:
        y = _rms(y, fg_ref[...])
    out_ref[...] = y


def _ffn(x2, ya, yb, yc, wo, g, wup, cw, cb, wdn, fg, layer, tm, seq, final_norm):
    t = x2.shape[0]
    row = lambda n: pl.BlockSpec((tm, n), lambda i: (i, 0))
    nch2 = 2 * D_FF // FFN_CHUNK
    body = functools.partial(_ffn_body, tiles_per_seq=seq // tm, final_norm=final_norm)
    return pl.pallas_call(
        body,
        grid=(t // tm,),
        in_specs=[row(D_MODEL), row(CONV_CH), row(MOBA_WIDTH), row(GLA_WIDTH), _resident_layer(wo, layer),
                  _resident(g.shape), _resident_layer(wup, layer), _resident(cw.shape),
                  _resident(cb.shape), _resident_layer(wdn, layer), _resident(fg.shape)],
        out_specs=row(D_MODEL),
        out_shape=jax.ShapeDtypeStruct((t, D_MODEL), F32),
        scratch_shapes=[pltpu.VMEM((tm, D_MODEL), BF16),
                        pltpu.VMEM((tm, D_MODEL), F32),
                        pltpu.VMEM((2, FFN_CHUNK // LANES, SUBLANES + tm, LANES), F32),
                        pltpu.VMEM((2, FFN_CHUNK // LANES, SUBLANES + tm, LANES), F32),
                        pltpu.VMEM((nch2, SUBLANES, FFN_CHUNK), F32),
                        pltpu.VMEM((tm, 2 * FFN_CHUNK), BF16)],
        compiler_params=_params(),
        name="ffn",
    )(x2, ya, yb, yc, wo, g, wup, cw, cb, wdn, fg)


_PREP_STEPS = 8


def _prep_body(win_t_ref, wout_ref, wup_ref, wdn_ref, win_o, wout_o, wup_o, wdn_o):
    @pl.when(pl.program_id(1) == 0)
    def _():
        o_g = _C_R
        o_r = o_g + GLA_GATE_RANK
        win_o[0:o_g, :] = win_t_ref[0:o_g, :].astype(BF16)
        win_o[_C_R:_C_G, :] = win_t_ref[o_r:, :].astype(BF16)
        win_o[_C_G:_C_G + GLA_GATE_RANK, :] = win_t_ref[o_g:o_r, :].astype(BF16)
        win_o[_C_G + GLA_GATE_RANK:, :] = jnp.zeros(
            (_IN_COLS_PAD - _C_G - GLA_GATE_RANK, win_o.shape[1]), BF16)

    wout_o[...] = wout_ref[...].astype(BF16)
    wup_o[...] = wup_ref[...].astype(BF16)
    wdn_o[...] = wdn_ref[...].astype(BF16)


def _prep_weights(w_in_t, w_out, w_up, w_dn):
    depth, _, d = w_in_t.shape

    def slab(arr):
        return pl.BlockSpec((None, arr.shape[1] // _PREP_STEPS, arr.shape[2]), lambda l, i: (l, i, 0))

    whole = lambda rows: pl.BlockSpec((None, rows, d), lambda l, i: (l, 0, 0))
    outs = [(_IN_COLS_PAD, d), w_out.shape[1:], w_up.shape[1:], w_dn.shape[1:]]
    return pl.pallas_call(
        _prep_body,
        grid=(depth, _PREP_STEPS),
        in_specs=[whole(w_in_t.shape[1]), slab(w_out), slab(w_up), slab(w_dn)],
        out_specs=[whole(_IN_COLS_PAD), slab(w_out), slab(w_up), slab(w_dn)],
        out_shape=[jax.ShapeDtypeStruct((depth,) + tuple(s), BF16) for s in outs],
        compiler_params=_params(2),
        name="weight_cast",
    )(w_in_t, w_out, w_up, w_dn)


def kernel(x, norm_mix_g, w_in, conv_w, conv_b, conv_ln_g, conv_ln_b, moba_out_g,
           gla_gate_w, gla_gate_b, gla_out_g, w_out, norm_ffn_g, ffn_w_up, ffn_conv_w,
           ffn_conv_b, ffn_w_down, final_g):
    bsz, seq, d = x.shape
    depth = w_in.shape[0]
    x2 = x.reshape(bsz * seq, d)
    row = lambda v: v.reshape(1, -1).astype(F32)
    w_in_p, w_out_p, w_up_p, w_dn_p = _prep_weights(jnp.swapaxes(w_in, 1, 2), w_out, ffn_w_up, ffn_w_down)
    for l in range(depth):
        gw = jnp.concatenate([gla_gate_w[l], jnp.zeros((LANES - GLA_GATE_RANK, GLA_HEADS * GLA_DK), F32)],
                             axis=0).astype(BF16)
        ya, qkv, gq, gk, gv, gr, la = _in_proj(
            x2, row(norm_mix_g[l]), w_in_p, l, gw, row(gla_gate_b[l]), conv_w[l], row(conv_b[l]),
            row(conv_ln_g[l]), row(conv_ln_b[l]), _IN_PROJ_TILE, seq)
        yb = _moba(qkv, row(moba_out_g[l]), seq)
        yc = _gla(gq, gk, gv, la, gr, row(gla_out_g[l]), seq)
        x2 = _ffn(x2, ya, yb, yc, w_out_p, row(norm_ffn_g[l]), w_up_p, ffn_conv_w[l],
                  row(ffn_conv_b[l]), w_dn_p, row(final_g), l, _FFN_TILE, seq,
                  final_norm=(l == depth - 1))
    return x2.reshape(bsz, seq, d)
```

```python
import functools

import jax
import jax.numpy as jnp
from jax import lax
from jax.experimental import pallas as pl
from jax.experimental.pallas import tpu as pltpu

F32 = jnp.float32
BF16 = jnp.bfloat16

D_MODEL = 1024
CONV_CH = 256
CONV_WIDTH = 31
MOBA_HEADS = 4
MOBA_DH = 64
MOBA_WIDTH = 256
MOBA_BLOCK = 256
MOBA_TOPK = 3
GLA_HEADS = 4
GLA_DK = 64
GLA_DV = 128
GLA_WIDTH = 512
GLA_GATE_RANK = 16
GLA_TAU = 16.0
GLA_BLOCK = 256
GLA_SUB = 32
D_FF = 2816
FFN_CHUNK = 256
FFN_CONV_WIDTH = 3
NORM_EPS = 1e-6
LN_EPS = 1e-5
NEG_INF = -1e30
LANES = 128
SUBLANES = 8
V7X_VMEM_BYTES = 64 * 1024 * 1024
VMEM_LIMIT = V7X_VMEM_BYTES - 8 * 1024 * 1024

_C_A = 0
_C_B = _C_A + 2 * CONV_CH
_C_Q = _C_B + 3 * MOBA_WIDTH
_C_K = _C_Q + GLA_HEADS * GLA_DK
_C_V = _C_K + GLA_HEADS * GLA_DK
_C_R = _C_V + GLA_WIDTH
_C_G = _C_R + GLA_WIDTH
_IN_COLS_PAD = _C_G + LANES


def _sigmoid(x):
    return 1.0 / (1.0 + jnp.exp(-x))


def _rms(x, g):
    return x * lax.rsqrt(jnp.mean(x * x, axis=-1, keepdims=True) + NORM_EPS) * g


def _nt(a, b):
    return lax.dot_general(a, b, (((1,), (1,)), ((), ())), preferred_element_type=F32)


def _params(n_axes=1):
    return pltpu.CompilerParams(dimension_semantics=("arbitrary",) * n_axes,
                                vmem_limit_bytes=VMEM_LIMIT)


def _resident_layer(stacked, layer):
    tail = stacked.shape[1:]
    return pl.BlockSpec((None,) + tail, lambda *_: (layer,) + (0,) * len(tail),
                        pipeline_mode=pl.Buffered(1))


_IN_PROJ_TILE = 1024
_IN_PROJ_SUB = 512
_CONV_PAD = 32
_CONV_ROWS = 64


def _in_proj_body(x_ref, g_ref, w_ref, gw_ref, gb_ref, cw_ref, cb_ref, lg_ref, lb_ref,
                  ya_ref, qkv_ref, gq_ref, gk_ref, gv_ref, gr_ref, la_ref, hp_ref, *, tiles_per_seq):
    tm = x_ref.shape[0]
    halves = CONV_CH // LANES
    first = pl.program_id(0) % tiles_per_seq == 0

    @pl.when(first)
    def _():
        hp_ref[:, 0:_CONV_PAD, :] = jnp.zeros((halves, _CONV_PAD, LANES), F32)

    @pl.when(jnp.logical_not(first))
    def _():
        hp_ref[:, 0:_CONV_PAD, :] = hp_ref[:, tm:tm + _CONV_PAD, :]

    gate_w = jnp.concatenate(
        [gw_ref[...], jnp.zeros((LANES - GLA_GATE_RANK, gw_ref.shape[1]), F32)], axis=0).astype(BF16)

    def norm(rows):
        return _rms(x_ref[rows, :], g_ref[...]).astype(BF16)

    def conformer(r0):
        accs = []
        for c in range(halves):
            cols = slice(c * LANES, (c + 1) * LANES)
            acc = jnp.broadcast_to(cb_ref[:, cols], (_CONV_ROWS, LANES))
            for i in range(CONV_WIDTH):
                off = r0 + _CONV_PAD - (CONV_WIDTH - 1) + i
                acc = acc + cw_ref[i:i + 1, cols] * hp_ref[c, off:off + _CONV_ROWS, :]
            accs.append(acc)
        y = jnp.concatenate(accs, axis=1)
        d = y - jnp.mean(y, axis=-1, keepdims=True)
        var = jnp.mean(d * d, axis=-1, keepdims=True)
        y = d * lax.rsqrt(var + LN_EPS) * lg_ref[...] + lb_ref[...]
        ya_ref[r0:r0 + _CONV_ROWS, :] = (y * _sigmoid(y)).astype(ya_ref.dtype)

    def project(rows, h):
        def mm(lo, hi):
            return _nt(h, w_ref[lo:hi, :])

        a = mm(_C_A, _C_B)
        glu = a[:, :CONV_CH] * _sigmoid(a[:, CONV_CH:])
        for c in range(halves):
            hp_ref[c, _CONV_PAD + rows.start:_CONV_PAD + rows.stop, :] = glu[:, c * LANES:(c + 1) * LANES]
        def moba_qkv():
            qkv_ref[rows, :] = mm(_C_B, _C_Q).astype(BF16)

        def gla_q():
            gq_ref[rows, :] = mm(_C_Q, _C_K) * (GLA_DK ** -0.5)

        def gla_k():
            gk_ref[rows, :] = mm(_C_K, _C_V)

        def gla_v():
            gv_ref[rows, :] = mm(_C_V, _C_R).astype(BF16)

        def gla_r():
            r = mm(_C_R, _C_G)
            gr_ref[rows, :] = (r * _sigmoid(r)).astype(BF16)

        def gla_gate():
            z = jnp.dot(mm(_C_G, _IN_COLS_PAD).astype(BF16), gate_w,
                        preferred_element_type=F32) + gb_ref[...]
            la_ref[rows, :] = (jnp.minimum(z, 0.0) - jnp.log1p(jnp.exp(-jnp.abs(z)))) * (1.0 / GLA_TAU)

        groups = [moba_qkv, gla_q, gla_k, gla_v, gla_r, gla_gate]
        passes = list(range(rows.start, rows.stop, _CONV_ROWS))
        per = -(-len(passes) // len(groups))
        for gi, group in enumerate(groups):
            group()
            for r0 in passes[gi * per:(gi + 1) * per]:
                conformer(r0)

    parts = [slice(r0, r0 + _IN_PROJ_SUB) for r0 in range(0, tm, _IN_PROJ_SUB)]
    h_next = norm(parts[0])
    for k, rows in enumerate(parts):
        h = h_next
        if k + 1 < len(parts):
            h_next = norm(parts[k + 1])
        project(rows, h)


def _in_proj(x2, g, w, layer, gw, gb, cw, cb, lg, lb, tm, seq):
    t = x2.shape[0]
    row = lambda n: pl.BlockSpec((tm, n), lambda i: (i, 0))
    outs = [(CONV_CH, BF16), (3 * MOBA_WIDTH, BF16), (GLA_HEADS * GLA_DK, F32), (GLA_HEADS * GLA_DK, F32),
            (GLA_WIDTH, BF16), (GLA_WIDTH, BF16), (GLA_HEADS * GLA_DK, F32)]
    return pl.pallas_call(
        functools.partial(_in_proj_body, tiles_per_seq=seq // tm),
        grid=(t // tm,),
        in_specs=[row(D_MODEL)] + [_resident_layer(p, layer) for p in (g, w, gw, gb, cw, cb, lg, lb)],
        out_specs=[row(n) for n, _ in outs],
        out_shape=[jax.ShapeDtypeStruct((t, n), d) for n, d in outs],
        scratch_shapes=[pltpu.VMEM((CONV_CH // LANES, _CONV_PAD + tm, LANES), F32)],
        compiler_params=_params(),
        name="in_proj",
    )(x2, g, w, gw, gb, cw, cb, lg, lb)


_LOG2E = 1.4426950408889634


def _moba_body(q_ref, k_ref, v_ref, g_ref, out_ref, kaug_ref, qaug_t_ref, vaug_t_ref,
               s_a_ref, s_b_ref, e_a_ref, e_b_ref):
    seq = q_ref.shape[0]
    nb = seq // MOBA_BLOCK
    blk = MOBA_BLOCK
    dh = MOBA_DH
    aug = 2 * dh

    row_blk = lax.broadcasted_iota(jnp.int32, (seq, dh), 0) // blk
    lane = lax.broadcasted_iota(jnp.int32, (seq, dh), 1)
    onehot = jnp.where(lane == row_blk, 1.0, 0.0).astype(BF16)
    n_idx = lax.broadcasted_iota(jnp.int32, (nb, seq), 0)
    own = lax.broadcasted_iota(jnp.int32, (nb, seq), 1) // blk
    ones_row = jnp.where(lax.broadcasted_iota(jnp.int32, (SUBLANES, seq), 0) == 0, 1.0, 0.0)
    q_t = q_ref[...].astype(F32).T
    v_t = v_ref[...].astype(F32).T
    for h in range(MOBA_HEADS):
        hs = slice(h * dh, (h + 1) * dh)
        kh = k_ref[:, hs]
        kaug_ref[h] = jnp.concatenate([kh, onehot], axis=1)
        km = jnp.sum(kh.astype(F32).reshape(nb, blk, dh), axis=1) * (1.0 / blk)
        km_hi = km.astype(BF16)
        km_lo = (km - km_hi.astype(F32)).astype(BF16)
        qh_t = q_t[hs, :]
        qh_t16 = qh_t.astype(BF16)
        gate = (jnp.dot(km_hi, qh_t16, preferred_element_type=F32)
                + jnp.dot(km_lo, qh_t16, preferred_element_type=F32))
        gm = jnp.where(n_idx < own, gate, NEG_INF)
        rank = jnp.zeros((nb, seq), F32)
        for m in range(nb):
            gm_m = gm[m:m + 1, :]
            beats = (gm_m > gm) | ((gm_m == gm) & (m < n_idx))
            rank = rank + jnp.where(beats, 1.0, 0.0)
        keep = ((rank < MOBA_TOPK) & (n_idx < own)) | (n_idx >= own)
        bias = jnp.where(keep, 0.0, NEG_INF)
        qaug_t_ref[h] = jnp.concatenate(
            [qh_t * (dh ** -0.5 * _LOG2E), bias, jnp.zeros((aug - dh - nb, seq), F32)], axis=0).astype(BF16)
        vaug_t_ref[h] = jnp.concatenate(
            [v_t[hs, :], ones_row, jnp.zeros((aug - dh - SUBLANES, seq), F32)], axis=0).astype(BF16)

    causal = (lax.broadcasted_iota(jnp.int32, (blk, blk), 0)
              <= lax.broadcasted_iota(jnp.int32, (blk, blk), 1))

    def scores(i, h, s_ref):
        cur = slice(i * blk, (i + 1) * blk)
        qa = qaug_t_ref[h, :, cur]
        s_own = jnp.dot(kaug_ref[h, cur, :], qa, preferred_element_type=F32)
        s_own = jnp.where(causal, s_own, NEG_INF)
        s_ref[cur, :] = s_own
        m_col = jnp.max(s_own, axis=0, keepdims=True)
        if i > 0:
            s_past = jnp.dot(kaug_ref[h, 0:i * blk, :], qa, preferred_element_type=F32)
            s_ref[0:i * blk, :] = s_past
            m_col = jnp.maximum(m_col, jnp.max(s_past, axis=0, keepdims=True))
        return m_col

    def weights(i, s_ref, m_col, e_ref):
        keys = slice(0, (i + 1) * blk)
        e_ref[keys, :] = jnp.exp2(s_ref[keys, :] - m_col).astype(BF16)

    def finish(i, h, e_ref):
        keys = slice(0, (i + 1) * blk)
        o = jnp.dot(vaug_t_ref[h, :, keys], e_ref[keys, :], preferred_element_type=F32)
        o = o[0:dh, :] / o[dh:dh + 1, :]
        return o * lax.rsqrt(jnp.mean(o * o, axis=0, keepdims=True) + NORM_EPS)

    units = [(i, h) for i in range(nb) for h in range(MOBA_HEADS)]
    s_bufs = (s_a_ref, s_b_ref)
    e_bufs = (e_a_ref, e_b_ref)
    m_cols = {0: scores(*units[0], s_bufs[0]), 1: scores(*units[1], s_bufs[1])}
    weights(units[0][0], s_bufs[0], m_cols.pop(0), e_bufs[0])
    outs = []
    for u, (i, h) in enumerate(units):
        if u + 2 < len(units):
            m_cols[u + 2] = scores(*units[u + 2], s_bufs[u % 2])
        if u + 1 < len(units):
            weights(units[u + 1][0], s_bufs[(u + 1) % 2], m_cols.pop(u + 1), e_bufs[(u + 1) % 2])
        outs.append(finish(i, h, e_bufs[u % 2]))
        if h == MOBA_HEADS - 1:
            y = jnp.concatenate(outs, axis=0).T
            out_ref[i * blk:(i + 1) * blk, :] = (y * g_ref[...]).astype(out_ref.dtype)
            outs = []


def _moba(qkv, g, layer, seq):
    t = qkv.shape[0]
    col = lambda c: pl.BlockSpec((seq, MOBA_WIDTH), lambda b: (b, c))
    return pl.pallas_call(
        _moba_body,
        grid=(t // seq,),
        in_specs=[col(0), col(1), col(2), _resident_layer(g, layer)],
        out_specs=pl.BlockSpec((seq, MOBA_WIDTH), lambda b: (b, 0)),
        out_shape=jax.ShapeDtypeStruct((t, MOBA_WIDTH), BF16),
        scratch_shapes=[pltpu.VMEM((MOBA_HEADS, seq, 2 * MOBA_DH), BF16),
                        pltpu.VMEM((MOBA_HEADS, 2 * MOBA_DH, seq), BF16),
                        pltpu.VMEM((MOBA_HEADS, 2 * MOBA_DH, seq), BF16),
                        pltpu.VMEM((seq, MOBA_BLOCK), F32),
                        pltpu.VMEM((seq, MOBA_BLOCK), F32),
                        pltpu.VMEM((seq, MOBA_BLOCK), BF16),
                        pltpu.VMEM((seq, MOBA_BLOCK), BF16)],
        compiler_params=_params(),
        name="moba",
    )(qkv, qkv, qkv, g)


def _split3(x):
    hi = x.astype(BF16)
    r = x - hi.astype(F32)
    mid = r.astype(BF16)
    lo = (r - mid.astype(F32)).astype(BF16)
    return hi, mid, lo


def _gla_body(q_ref, k_ref, v_ref, la_ref, r_ref, g_ref, out_ref, st_ref):
    blk = GLA_BLOCK
    sub = GLA_SUB
    nsub = blk // sub
    dk = GLA_DK
    dv = GLA_DV

    @pl.when(pl.program_id(1) == 0)
    def _():
        st_ref[...] = jnp.zeros(st_ref.shape, F32)

    nblk = q_ref.shape[0] // blk
    ri = lax.broadcasted_iota(jnp.int32, (blk, blk), 0)
    ci = lax.broadcasted_iota(jnp.int32, (blk, blk), 1)
    tril = jnp.where(ci <= ri, 1.0, 0.0).astype(BF16)
    causal = ci <= ri
    b_all = [sum(jnp.dot(tril, part, preferred_element_type=F32)
                 for part in _split3(la_ref[n * blk:(n + 1) * blk, :])) for n in range(nblk)]

    pair = LANES // dk
    lane_head = lax.broadcasted_iota(jnp.int32, (1, LANES), 1) // dk

    def prep(n, p):
        ls = slice(p * LANES, (p + 1) * LANES)
        rows = slice(n * blk, (n + 1) * blk)
        q = q_ref[rows, ls]
        k = k_ref[rows, ls]
        b = b_all[n][:, ls]
        s = [jnp.zeros((1, LANES), F32)] + [b[I * sub - 1:I * sub, :] for I in range(1, nsub)]
        s_row = jnp.concatenate([jnp.broadcast_to(si, (sub, LANES)) for si in s], axis=0)
        b_end = b[blk - 1:blk, :]
        qt = (q * jnp.exp(b - s_row)).astype(BF16)
        q_in = (q * jnp.exp(b)).astype(BF16)
        k_out = (k * jnp.exp(b_end - b)).astype(BF16)
        k_sub = []
        for I in range(nsub):
            nk = (I + 1) * sub
            k_i = (k[0:nk, :] * jnp.exp(s[I] - b[0:nk, :])).astype(BF16)
            if nk < blk:
                k_i = jnp.concatenate([k_i, jnp.zeros((blk - nk, LANES), BF16)], axis=0)
            k_sub.append(k_i)
        heads = []
        for j in range(pair):
            mine = lane_head == j
            qt_h = jnp.where(mine, qt, 0.0)
            a = jnp.concatenate([_nt(qt_h[I * sub:(I + 1) * sub, :], k_sub[I]) for I in range(nsub)], axis=0)
            a = jnp.where(causal, a, 0.0).astype(BF16)
            heads.append((a, jnp.where(mine, q_in, 0.0), mine))
        return heads, k_out, jnp.exp(b_end)

    def finish(n, h, a, q_in, mine, k_out, decay):
        vs = slice(h * dv, (h + 1) * dv)
        rows = slice(n * blk, (n + 1) * blk)
        v = v_ref[rows, vs]
        st = st_ref[h]
        o = jnp.dot(a, v, preferred_element_type=F32) + _nt(q_in, st.astype(BF16))
        v_t = v.astype(F32).T.astype(BF16)
        st_ref[h] = jnp.where(mine, st * decay + jnp.dot(v_t, k_out, preferred_element_type=F32), 0.0)
        o = o * lax.rsqrt(jnp.mean(o * o, axis=-1, keepdims=True) + NORM_EPS) * g_ref[:, vs]
        out_ref[rows, vs] = (o * r_ref[rows, vs].astype(F32)).astype(out_ref.dtype)

    units = [(n, p) for n in range(nblk) for p in range(GLA_HEADS // pair)]
    nxt = prep(*units[0])
    for u, (n, p) in enumerate(units):
        heads, k_out, decay = nxt
        if u + 1 < len(units):
            nxt = prep(*units[u + 1])
        for j, (a, q_in, mine) in enumerate(heads):
            finish(n, p * pair + j, a, q_in, mine, k_out, decay)


_GLA_STEP_BLOCKS = 4


def _gla(gq, gk, gv, la, gr, g, layer, seq):
    t = gq.shape[0]
    rows = GLA_BLOCK * _GLA_STEP_BLOCKS
    nblk = seq // rows
    blk = lambda n: pl.BlockSpec((rows, n), lambda b, i: (b * nblk + i, 0))
    kw = GLA_HEADS * GLA_DK
    return pl.pallas_call(
        _gla_body,
        grid=(t // seq, nblk),
        in_specs=[blk(kw), blk(kw), blk(GLA_WIDTH), blk(kw), blk(GLA_WIDTH), _resident_layer(g, layer)],
        out_specs=blk(GLA_WIDTH),
        out_shape=jax.ShapeDtypeStruct((t, GLA_WIDTH), BF16),
        scratch_shapes=[pltpu.VMEM((GLA_HEADS, GLA_DV, LANES), F32)],
        compiler_params=_params(2),
        name="gla",
    )(gq, gk, gv, la, gr, g)


_FFN_TILE = 512


def _ffn_body(x_ref, ya_ref, yb_ref, yc_ref, wo_ref, g_ref, wup_ref, cw_ref, cb_ref, wdn_ref,
              fg_ref, out_ref, h_ref, acc_ref, ubuf_a, ubuf_b, carry_ref, abuf_ref,
              *, tiles_per_seq, final_norm):
    tm = x_ref.shape[0]
    nch = D_FF // FFN_CHUNK
    halves = FFN_CHUNK // LANES
    taps = FFN_CONV_WIDTH
    o_b = CONV_CH
    o_c = CONV_CH + MOBA_WIDTH

    @pl.when(pl.program_id(0) % tiles_per_seq == 0)
    def _():
        carry_ref[...] = jnp.zeros(carry_ref.shape, F32)

    y = jnp.dot(ya_ref[...], wo_ref[0:o_b, :], preferred_element_type=F32)
    y = y + jnp.dot(yb_ref[...], wo_ref[o_b:o_c, :], preferred_element_type=F32)
    y = y + jnp.dot(yc_ref[...], wo_ref[o_c:, :], preferred_element_type=F32)
    x1 = x_ref[...] + y
    out_ref[...] = x1
    h_ref[...] = _rms(x1, g_ref[...]).astype(BF16)
    acc_ref[...] = jnp.zeros(acc_ref.shape, F32)

    def up(j, ubuf):
        for s in range(2):
            lo = s * D_FF + j * FFN_CHUNK
            u = jnp.dot(h_ref[...], wup_ref[:, lo:lo + FFN_CHUNK], preferred_element_type=F32)
            for c in range(halves):
                ubuf[s, c, SUBLANES:, :] = u[:, c * LANES:(c + 1) * LANES]

    def gate_down(j, ubuf):
        ys = []
        for s in range(2):
            outs = []
            for c in range(halves):
                lo = s * D_FF + j * FFN_CHUNK + c * LANES
                cols = slice(lo, lo + LANES)
                hist = carry_ref.at[s * nch + j, :, c * LANES:(c + 1) * LANES]
                ubuf[s, c, 0:SUBLANES, :] = hist[...]
                y = cb_ref[:, cols]
                for d in range(taps):
                    y = y + cw_ref[taps - 1 - d:taps - d, cols] * ubuf[s, c, pl.ds(SUBLANES - d, tm), :]
                hist[...] = ubuf[s, c, tm:tm + SUBLANES, :]
                outs.append(y)
            ys.append(jnp.concatenate(outs, axis=1))
        val, gate = ys
        a = (val * _sigmoid(val) * gate).astype(BF16)
        if j % 2 == 0 and j + 1 < nch:
            abuf_ref[:, 0:FFN_CHUNK] = a
        elif j % 2 == 1:
            abuf_ref[:, FFN_CHUNK:] = a
            acc_ref[...] += jnp.dot(abuf_ref[...], wdn_ref[(j - 1) * FFN_CHUNK:(j + 1) * FFN_CHUNK, :],
                                    preferred_element_type=F32)
        else:
            acc_ref[...] += jnp.dot(a, wdn_ref[j * FFN_CHUNK:(j + 1) * FFN_CHUNK, :],
                                    preferred_element_type=F32)

    assert nch % 2 == 1
    up(0, ubuf_a)
    for j in range(0, nch - 1, 2):
        up(j + 1, ubuf_b)
        gate_down(j, ubuf_a)
        up(j + 2, ubuf_a)
        gate_down(j + 1, ubuf_b)
    gate_down(nch - 1, ubuf_a)
    y = out_ref[...] + acc_ref[...]
    if final_norm:
        y = _rms(y, fg_ref[...])
    out_ref[...] = y


def _ffn(x2, ya, yb, yc, wo, g, wup, cw, cb, wdn, fg, layer, tm, seq, final_norm):
    t = x2.shape[0]
    row = lambda n: pl.BlockSpec((tm, n), lambda i: (i, 0))
    nch2 = 2 * D_FF // FFN_CHUNK
    body = functools.partial(_ffn_body, tiles_per_seq=seq // tm, final_norm=final_norm)
    return pl.pallas_call(
        body,
        grid=(t // tm,),
        in_specs=([row(D_MODEL), row(CONV_CH), row(MOBA_WIDTH), row(GLA_WIDTH)]
                  + [_resident_layer(p, layer) for p in (wo, g, wup, cw, cb, wdn)]
                  + [_resident_layer(fg, 0)]),
        out_specs=row(D_MODEL),
        out_shape=jax.ShapeDtypeStruct((t, D_MODEL), F32),
        scratch_shapes=[pltpu.VMEM((tm, D_MODEL), BF16),
                        pltpu.VMEM((tm, D_MODEL), F32),
                        pltpu.VMEM((2, FFN_CHUNK // LANES, SUBLANES + tm, LANES), F32),
                        pltpu.VMEM((2, FFN_CHUNK // LANES, SUBLANES + tm, LANES), F32),
                        pltpu.VMEM((nch2, SUBLANES, FFN_CHUNK), F32),
                        pltpu.VMEM((tm, 2 * FFN_CHUNK), BF16)],
        compiler_params=_params(),
        name="ffn",
    )(x2, ya, yb, yc, wo, g, wup, cw, cb, wdn, fg)


_PREP_STEPS = 8


def _prep_body(win_t_ref, wout_ref, wup_ref, wdn_ref, win_o, wout_o, wup_o, wdn_o):
    @pl.when(pl.program_id(1) == 0)
    def _():
        o_g = _C_R
        o_r = o_g + GLA_GATE_RANK
        win_o[0:o_g, :] = win_t_ref[0:o_g, :].astype(BF16)
        win_o[_C_R:_C_G, :] = win_t_ref[o_r:, :].astype(BF16)
        win_o[_C_G:_C_G + GLA_GATE_RANK, :] = win_t_ref[o_g:o_r, :].astype(BF16)
        win_o[_C_G + GLA_GATE_RANK:, :] = jnp.zeros(
            (_IN_COLS_PAD - _C_G - GLA_GATE_RANK, win_o.shape[1]), BF16)

    wout_o[...] = wout_ref[...].astype(BF16)
    wup_o[...] = wup_ref[...].astype(BF16)
    wdn_o[...] = wdn_ref[...].astype(BF16)


def _prep_weights(w_in_t, w_out, w_up, w_dn):
    depth, _, d = w_in_t.shape

    def slab(arr):
        return pl.BlockSpec((None, arr.shape[1] // _PREP_STEPS, arr.shape[2]), lambda l, i: (l, i, 0))

    whole = lambda rows: pl.BlockSpec((None, rows, d), lambda l, i: (l, 0, 0))
    outs = [(_IN_COLS_PAD, d), w_out.shape[1:], w_up.shape[1:], w_dn.shape[1:]]
    return pl.pallas_call(
        _prep_body,
        grid=(depth, _PREP_STEPS),
        in_specs=[whole(w_in_t.shape[1]), slab(w_out), slab(w_up), slab(w_dn)],
        out_specs=[whole(_IN_COLS_PAD), slab(w_out), slab(w_up), slab(w_dn)],
        out_shape=[jax.ShapeDtypeStruct((depth,) + tuple(s), BF16) for s in outs],
        compiler_params=_params(2),
        name="weight_cast",
    )(w_in_t, w_out, w_up, w_dn)


def kernel(x, norm_mix_g, w_in, conv_w, conv_b, conv_ln_g, conv_ln_b, moba_out_g,
           gla_gate_w, gla_gate_b, gla_out_g, w_out, norm_ffn_g, ffn_w_up, ffn_conv_w,
           ffn_conv_b, ffn_w_down, final_g):
    bsz, seq, d = x.shape
    depth = w_in.shape[0]
    x2 = x.reshape(bsz * seq, d)
    rows = lambda v: v.reshape(v.shape[0], 1, -1)
    w_in_p, w_out_p, w_up_p, w_dn_p = _prep_weights(jnp.swapaxes(w_in, 1, 2), w_out, ffn_w_up, ffn_w_down)
    for l in range(depth):
        ya, qkv, gq, gk, gv, gr, la = _in_proj(
            x2, rows(norm_mix_g), w_in_p, l, gla_gate_w, rows(gla_gate_b), conv_w, rows(conv_b),
            rows(conv_ln_g), rows(conv_ln_b), _IN_PROJ_TILE, seq)
        yb = _moba(qkv, rows(moba_out_g), l, seq)
        yc = _gla(gq, gk, gv, la, gr, rows(gla_out_g), l, seq)
        x2 = _ffn(x2, ya, yb, yc, w_out_p, rows(norm_ffn_g), w_up_p, ffn_conv_w,
                  rows(ffn_conv_b), w_dn_p, rows(final_g[None]), l, _FFN_TILE, seq,
                  final_norm=(l == depth - 1))
    return x2.reshape(bsz, seq, d)
```

```python
import functools

import jax
import jax.numpy as jnp
from jax import lax
from jax.experimental import pallas as pl
from jax.experimental.pallas import tpu as pltpu

F32 = jnp.float32
BF16 = jnp.bfloat16

D_MODEL = 1024
CONV_CH = 256
CONV_WIDTH = 31
MOBA_HEADS = 4
MOBA_DH = 64
MOBA_WIDTH = 256
MOBA_BLOCK = 256
MOBA_TOPK = 3
GLA_HEADS = 4
GLA_DK = 64
GLA_DV = 128
GLA_WIDTH = 512
GLA_GATE_RANK = 16
GLA_TAU = 16.0
GLA_BLOCK = 256
GLA_SUB = 32
D_FF = 2816
FFN_CHUNK = 256
FFN_CONV_WIDTH = 3
NORM_EPS = 1e-6
LN_EPS = 1e-5
NEG_INF = -1e30
LANES = 128
SUBLANES = 8
V7X_VMEM_BYTES = 64 * 1024 * 1024
VMEM_LIMIT = V7X_VMEM_BYTES - 8 * 1024 * 1024

_C_A = 0
_C_B = _C_A + 2 * CONV_CH
_C_Q = _C_B + 3 * MOBA_WIDTH
_C_K = _C_Q + GLA_HEADS * GLA_DK
_C_V = _C_K + GLA_HEADS * GLA_DK
_C_R = _C_V + GLA_WIDTH
_C_G = _C_R + GLA_WIDTH
_IN_COLS_PAD = _C_G + LANES


def _sigmoid(x):
    return 1.0 / (1.0 + jnp.exp(-x))


def _rms(x, g):
    return x * lax.rsqrt(jnp.mean(x * x, axis=-1, keepdims=True) + NORM_EPS) * g


def _nt(a, b):
    return lax.dot_general(a, b, (((1,), (1,)), ((), ())), preferred_element_type=F32)


def _params(n_axes=1):
    return pltpu.CompilerParams(dimension_semantics=("arbitrary",) * n_axes,
                                vmem_limit_bytes=VMEM_LIMIT)


def _resident_layer(stacked, layer):
    tail = stacked.shape[1:]
    return pl.BlockSpec((None,) + tail, lambda *_: (layer,) + (0,) * len(tail),
                        pipeline_mode=pl.Buffered(1))


_IN_PROJ_TILE = 1024
_IN_PROJ_SUB = 512
_CONV_PAD = 32
_CONV_ROWS = 64


_N_IN_PROJ_INS = 9
_N_IN_PROJ_OUTS = 7


def _in_proj_body(*refs, tiles_per_seq, n_cast):
    x_ref, g_ref, w_ref, gw_ref, gb_ref, cw_ref, cb_ref, lg_ref, lb_ref = refs[:_N_IN_PROJ_INS]
    outs = refs[_N_IN_PROJ_INS + n_cast:]
    ya_ref, qkv_ref, gq_ref, gk_ref, gv_ref, gr_ref, la_ref = outs[:_N_IN_PROJ_OUTS]
    hp_ref = outs[_N_IN_PROJ_OUTS + n_cast]
    for src, dst in zip(refs[_N_IN_PROJ_INS:_N_IN_PROJ_INS + n_cast],
                        outs[_N_IN_PROJ_OUTS:_N_IN_PROJ_OUTS + n_cast]):
        dst[...] = src[...].astype(BF16)
    tm = x_ref.shape[0]
    halves = CONV_CH // LANES
    first = pl.program_id(0) % tiles_per_seq == 0

    @pl.when(first)
    def _():
        hp_ref[:, 0:_CONV_PAD, :] = jnp.zeros((halves, _CONV_PAD, LANES), F32)

    @pl.when(jnp.logical_not(first))
    def _():
        hp_ref[:, 0:_CONV_PAD, :] = hp_ref[:, tm:tm + _CONV_PAD, :]

    gate_w = jnp.concatenate(
        [gw_ref[...], jnp.zeros((LANES - GLA_GATE_RANK, gw_ref.shape[1]), F32)], axis=0).astype(BF16)

    def norm(rows):
        return _rms(x_ref[rows, :], g_ref[...]).astype(BF16)

    def conformer(r0):
        accs = []
        for c in range(halves):
            cols = slice(c * LANES, (c + 1) * LANES)
            acc = jnp.broadcast_to(cb_ref[:, cols], (_CONV_ROWS, LANES))
            for i in range(CONV_WIDTH):
                off = r0 + _CONV_PAD - (CONV_WIDTH - 1) + i
                acc = acc + cw_ref[i:i + 1, cols] * hp_ref[c, off:off + _CONV_ROWS, :]
            accs.append(acc)
        y = jnp.concatenate(accs, axis=1)
        d = y - jnp.mean(y, axis=-1, keepdims=True)
        var = jnp.mean(d * d, axis=-1, keepdims=True)
        y = d * lax.rsqrt(var + LN_EPS) * lg_ref[...] + lb_ref[...]
        ya_ref[r0:r0 + _CONV_ROWS, :] = (y * _sigmoid(y)).astype(ya_ref.dtype)

    def project(rows, h):
        def mm(lo, hi):
            return _nt(h, w_ref[lo:hi, :])

        a = mm(_C_A, _C_B)
        glu = a[:, :CONV_CH] * _sigmoid(a[:, CONV_CH:])
        for c in range(halves):
            hp_ref[c, _CONV_PAD + rows.start:_CONV_PAD + rows.stop, :] = glu[:, c * LANES:(c + 1) * LANES]
        def moba_qkv():
            qkv_ref[rows, :] = mm(_C_B, _C_Q).astype(BF16)

        def gla_q():
            gq_ref[rows, :] = mm(_C_Q, _C_K) * (GLA_DK ** -0.5)

        def gla_k():
            gk_ref[rows, :] = mm(_C_K, _C_V)

        def gla_v():
            gv_ref[rows, :] = mm(_C_V, _C_R).astype(BF16)

        def gla_r():
            r = mm(_C_R, _C_G)
            gr_ref[rows, :] = (r * _sigmoid(r)).astype(BF16)

        def gla_gate():
            z = jnp.dot(mm(_C_G, _IN_COLS_PAD).astype(BF16), gate_w,
                        preferred_element_type=F32) + gb_ref[...]
            la_ref[rows, :] = (jnp.minimum(z, 0.0) - jnp.log1p(jnp.exp(-jnp.abs(z)))) * (1.0 / GLA_TAU)

        groups = [moba_qkv, gla_q, gla_k, gla_v, gla_r, gla_gate]
        passes = list(range(rows.start, rows.stop, _CONV_ROWS))
        per = -(-len(passes) // len(groups))
        for gi, group in enumerate(groups):
            group()
            for r0 in passes[gi * per:(gi + 1) * per]:
                conformer(r0)

    parts = [slice(r0, r0 + _IN_PROJ_SUB) for r0 in range(0, tm, _IN_PROJ_SUB)]
    h_next = norm(parts[0])
    for k, rows in enumerate(parts):
        h = h_next
        if k + 1 < len(parts):
            h_next = norm(parts[k + 1])
        project(rows, h)


def _in_proj(x2, g, w, layer, gw, gb, cw, cb, lg, lb, tm, seq, cast=()):
    t = x2.shape[0]
    steps = t // tm
    row = lambda n: pl.BlockSpec((tm, n), lambda i: (i, 0))
    outs = [(CONV_CH, BF16), (3 * MOBA_WIDTH, BF16), (GLA_HEADS * GLA_DK, F32), (GLA_HEADS * GLA_DK, F32),
            (GLA_WIDTH, BF16), (GLA_WIDTH, BF16), (GLA_HEADS * GLA_DK, F32)]

    def slab(arr):
        per_layer = steps // arr.shape[0]
        assert per_layer * arr.shape[0] == steps and arr.shape[1] % (per_layer * 2 * SUBLANES) == 0
        return pl.BlockSpec((None, arr.shape[1] // per_layer, arr.shape[2]),
                            lambda i: (i // per_layer, i % per_layer, 0))

    return pl.pallas_call(
        functools.partial(_in_proj_body, tiles_per_seq=seq // tm, n_cast=len(cast)),
        grid=(steps,),
        in_specs=([row(D_MODEL)] + [_resident_layer(p, layer) for p in (g, w, gw, gb, cw, cb, lg, lb)]
                  + [slab(a) for a in cast]),
        out_specs=[row(n) for n, _ in outs] + [slab(a) for a in cast],
        out_shape=([jax.ShapeDtypeStruct((t, n), d) for n, d in outs]
                   + [jax.ShapeDtypeStruct(a.shape, BF16) for a in cast]),
        scratch_shapes=[pltpu.VMEM((CONV_CH // LANES, _CONV_PAD + tm, LANES), F32)],
        compiler_params=_params(),
        name="in_proj",
    )(x2, g, w, gw, gb, cw, cb, lg, lb, *cast)


_LOG2E = 1.4426950408889634


def _moba_body(q_ref, k_ref, v_ref, g_ref, out_ref, kaug_ref, qaug_t_ref, vaug_t_ref,
               s_a_ref, s_b_ref, e_a_ref, e_b_ref):
    seq = q_ref.shape[0]
    nb = seq // MOBA_BLOCK
    blk = MOBA_BLOCK
    dh = MOBA_DH
    aug = 2 * dh

    row_blk = lax.broadcasted_iota(jnp.int32, (seq, dh), 0) // blk
    lane = lax.broadcasted_iota(jnp.int32, (seq, dh), 1)
    onehot = jnp.where(lane == row_blk, 1.0, 0.0).astype(BF16)
    n_idx = lax.broadcasted_iota(jnp.int32, (nb, seq), 0)
    own = lax.broadcasted_iota(jnp.int32, (nb, seq), 1) // blk
    ones_row = jnp.where(lax.broadcasted_iota(jnp.int32, (SUBLANES, seq), 0) == 0, 1.0, 0.0)
    q_t = q_ref[...].astype(F32).T
    v_t = v_ref[...].astype(F32).T
    for h in range(MOBA_HEADS):
        hs = slice(h * dh, (h + 1) * dh)
        kh = k_ref[:, hs]
        kaug_ref[h] = jnp.concatenate([kh, onehot], axis=1)
        km = jnp.sum(kh.astype(F32).reshape(nb, blk, dh), axis=1) * (1.0 / blk)
        km_hi = km.astype(BF16)
        km_lo = (km - km_hi.astype(F32)).astype(BF16)
        qh_t = q_t[hs, :]
        qh_t16 = qh_t.astype(BF16)
        gate = (jnp.dot(km_hi, qh_t16, preferred_element_type=F32)
                + jnp.dot(km_lo, qh_t16, preferred_element_type=F32))
        gm = jnp.where(n_idx < own, gate, NEG_INF)
        rank = jnp.zeros((nb, seq), F32)
        for m in range(nb):
            gm_m = gm[m:m + 1, :]
            beats = (gm_m > gm) | ((gm_m == gm) & (m < n_idx))
            rank = rank + jnp.where(beats, 1.0, 0.0)
        keep = ((rank < MOBA_TOPK) & (n_idx < own)) | (n_idx >= own)
        bias = jnp.where(keep, 0.0, NEG_INF)
        qaug_t_ref[h] = jnp.concatenate(
            [qh_t * (dh ** -0.5 * _LOG2E), bias, jnp.zeros((aug - dh - nb, seq), F32)], axis=0).astype(BF16)
        vaug_t_ref[h] = jnp.concatenate(
            [v_t[hs, :], ones_row, jnp.zeros((aug - dh - SUBLANES, seq), F32)], axis=0).astype(BF16)

    causal = (lax.broadcasted_iota(jnp.int32, (blk, blk), 0)
              <= lax.broadcasted_iota(jnp.int32, (blk, blk), 1))

    def scores(i, h, s_ref):
        cur = slice(i * blk, (i + 1) * blk)
        qa = qaug_t_ref[h, :, cur]
        s_own = jnp.dot(kaug_ref[h, cur, :], qa, preferred_element_type=F32)
        s_own = jnp.where(causal, s_own, NEG_INF)
        s_ref[cur, :] = s_own
        m_col = jnp.max(s_own, axis=0, keepdims=True)
        if i > 0:
            s_past = jnp.dot(kaug_ref[h, 0:i * blk, :], qa, preferred_element_type=F32)
            s_ref[0:i * blk, :] = s_past
            m_col = jnp.maximum(m_col, jnp.max(s_past, axis=0, keepdims=True))
        return m_col

    def weights(i, s_ref, m_col, e_ref):
        keys = slice(0, (i + 1) * blk)
        e_ref[keys, :] = jnp.exp2(s_ref[keys, :] - m_col).astype(BF16)

    def finish(i, h, e_ref):
        keys = slice(0, (i + 1) * blk)
        o = jnp.dot(vaug_t_ref[h, :, keys], e_ref[keys, :], preferred_element_type=F32)
        o = o[0:dh, :] / o[dh:dh + 1, :]
        return o * lax.rsqrt(jnp.mean(o * o, axis=0, keepdims=True) + NORM_EPS)

    units = [(i, h) for i in range(nb) for h in range(MOBA_HEADS)]
    s_bufs = (s_a_ref, s_b_ref)
    e_bufs = (e_a_ref, e_b_ref)
    m_cols = {0: scores(*units[0], s_bufs[0]), 1: scores(*units[1], s_bufs[1])}
    weights(units[0][0], s_bufs[0], m_cols.pop(0), e_bufs[0])
    outs = []
    for u, (i, h) in enumerate(units):
        if u + 2 < len(units):
            m_cols[u + 2] = scores(*units[u + 2], s_bufs[u % 2])
        if u + 1 < len(units):
            weights(units[u + 1][0], s_bufs[(u + 1) % 2], m_cols.pop(u + 1), e_bufs[(u + 1) % 2])
        outs.append(finish(i, h, e_bufs[u % 2]))
        if h == MOBA_HEADS - 1:
            y = jnp.concatenate(outs, axis=0).T
            out_ref[i * blk:(i + 1) * blk, :] = (y * g_ref[...]).astype(out_ref.dtype)
            outs = []


def _moba(qkv, g, layer, seq):
    t = qkv.shape[0]
    col = lambda c: pl.BlockSpec((seq, MOBA_WIDTH), lambda b: (b, c))
    return pl.pallas_call(
        _moba_body,
        grid=(t // seq,),
        in_specs=[col(0), col(1), col(2), _resident_layer(g, layer)],
        out_specs=pl.BlockSpec((seq, MOBA_WIDTH), lambda b: (b, 0)),
        out_shape=jax.ShapeDtypeStruct((t, MOBA_WIDTH), BF16),
        scratch_shapes=[pltpu.VMEM((MOBA_HEADS, seq, 2 * MOBA_DH), BF16),
                        pltpu.VMEM((MOBA_HEADS, 2 * MOBA_DH, seq), BF16),
                        pltpu.VMEM((MOBA_HEADS, 2 * MOBA_DH, seq), BF16),
                        pltpu.VMEM((seq, MOBA_BLOCK), F32),
                        pltpu.VMEM((seq, MOBA_BLOCK), F32),
                        pltpu.VMEM((seq, MOBA_BLOCK), BF16),
                        pltpu.VMEM((seq, MOBA_BLOCK), BF16)],
        compiler_params=_params(),
        name="moba",
    )(qkv, qkv, qkv, g)


def _split3(x):
    hi = x.astype(BF16)
    r = x - hi.astype(F32)
    mid = r.astype(BF16)
    lo = (r - mid.astype(F32)).astype(BF16)
    return hi, mid, lo


def _gla_body(q_ref, k_ref, v_ref, la_ref, r_ref, g_ref, out_ref, st_ref):
    blk = GLA_BLOCK
    sub = GLA_SUB
    nsub = blk // sub
    dk = GLA_DK
    dv = GLA_DV

    @pl.when(pl.program_id(1) == 0)
    def _():
        st_ref[...] = jnp.zeros(st_ref.shape, F32)

    nblk = q_ref.shape[0] // blk
    ri = lax.broadcasted_iota(jnp.int32, (blk, blk), 0)
    ci = lax.broadcasted_iota(jnp.int32, (blk, blk), 1)
    tril = jnp.where(ci <= ri, 1.0, 0.0).astype(BF16)
    causal = ci <= ri
    b_all = [sum(jnp.dot(tril, part, preferred_element_type=F32)
                 for part in _split3(la_ref[n * blk:(n + 1) * blk, :])) for n in range(nblk)]

    pair = LANES // dk
    lane_head = lax.broadcasted_iota(jnp.int32, (1, LANES), 1) // dk

    def prep(n, p):
        ls = slice(p * LANES, (p + 1) * LANES)
        rows = slice(n * blk, (n + 1) * blk)
        q = q_ref[rows, ls]
        k = k_ref[rows, ls]
        b = b_all[n][:, ls]
        s = [jnp.zeros((1, LANES), F32)] + [b[I * sub - 1:I * sub, :] for I in range(1, nsub)]
        s_row = jnp.concatenate([jnp.broadcast_to(si, (sub, LANES)) for si in s], axis=0)
        b_end = b[blk - 1:blk, :]
        qt = (q * jnp.exp(b - s_row)).astype(BF16)
        q_in = (q * jnp.exp(b)).astype(BF16)
        k_out = (k * jnp.exp(b_end - b)).astype(BF16)
        k_sub = []
        for I in range(nsub):
            nk = (I + 1) * sub
            k_i = (k[0:nk, :] * jnp.exp(s[I] - b[0:nk, :])).astype(BF16)
            if nk < blk:
                k_i = jnp.concatenate([k_i, jnp.zeros((blk - nk, LANES), BF16)], axis=0)
            k_sub.append(k_i)
        heads = []
        for j in range(pair):
            mine = lane_head == j
            qt_h = jnp.where(mine, qt, 0.0)
            a = jnp.concatenate([_nt(qt_h[I * sub:(I + 1) * sub, :], k_sub[I]) for I in range(nsub)], axis=0)
            a = jnp.where(causal, a, 0.0).astype(BF16)
            heads.append((a, jnp.where(mine, q_in, 0.0), mine))
        return heads, k_out, jnp.exp(b_end)

    def finish(n, h, a, q_in, mine, k_out, decay):
        vs = slice(h * dv, (h + 1) * dv)
        rows = slice(n * blk, (n + 1) * blk)
        v = v_ref[rows, vs]
        st = st_ref[h]
        o = jnp.dot(a, v, preferred_element_type=F32) + _nt(q_in, st.astype(BF16))
        v_t = v.astype(F32).T.astype(BF16)
        st_ref[h] = jnp.where(mine, st * decay + jnp.dot(v_t, k_out, preferred_element_type=F32), 0.0)
        o = o * lax.rsqrt(jnp.mean(o * o, axis=-1, keepdims=True) + NORM_EPS) * g_ref[:, vs]
        out_ref[rows, vs] = (o * r_ref[rows, vs].astype(F32)).astype(out_ref.dtype)

    units = [(n, p) for n in range(nblk) for p in range(GLA_HEADS // pair)]
    nxt = prep(*units[0])
    for u, (n, p) in enumerate(units):
        heads, k_out, decay = nxt
        if u + 1 < len(units):
            nxt = prep(*units[u + 1])
        for j, (a, q_in, mine) in enumerate(heads):
            finish(n, p * pair + j, a, q_in, mine, k_out, decay)


_GLA_STEP_BLOCKS = 4


def _gla(gq, gk, gv, la, gr, g, layer, seq):
    t = gq.shape[0]
    rows = GLA_BLOCK * _GLA_STEP_BLOCKS
    nblk = seq // rows
    blk = lambda n: pl.BlockSpec((rows, n), lambda b, i: (b * nblk + i, 0))
    kw = GLA_HEADS * GLA_DK
    return pl.pallas_call(
        _gla_body,
        grid=(t // seq, nblk),
        in_specs=[blk(kw), blk(kw), blk(GLA_WIDTH), blk(kw), blk(GLA_WIDTH), _resident_layer(g, layer)],
        out_specs=blk(GLA_WIDTH),
        out_shape=jax.ShapeDtypeStruct((t, GLA_WIDTH), BF16),
        scratch_shapes=[pltpu.VMEM((GLA_HEADS, GLA_DV, LANES), F32)],
        compiler_params=_params(2),
        name="gla",
    )(gq, gk, gv, la, gr, g)


_FFN_TILE = 512


def _ffn_body(x_ref, ya_ref, yb_ref, yc_ref, wo_ref, g_ref, wup_ref, cw_ref, cb_ref, wdn_ref,
              fg_ref, out_ref, h_ref, acc_ref, ubuf_a, ubuf_b, carry_ref, abuf_ref,
              *, tiles_per_seq, final_norm):
    tm = x_ref.shape[0]
    nch = D_FF // FFN_CHUNK
    halves = FFN_CHUNK // LANES
    taps = FFN_CONV_WIDTH
    o_b = CONV_CH
    o_c = CONV_CH + MOBA_WIDTH

    @pl.when(pl.program_id(0) % tiles_per_seq == 0)
    def _():
        carry_ref[...] = jnp.zeros(carry_ref.shape, F32)

    y = jnp.dot(ya_ref[...], wo_ref[0:o_b, :], preferred_element_type=F32)
    y = y + jnp.dot(yb_ref[...], wo_ref[o_b:o_c, :], preferred_element_type=F32)
    y = y + jnp.dot(yc_ref[...], wo_ref[o_c:, :], preferred_element_type=F32)
    x1 = x_ref[...] + y
    out_ref[...] = x1
    h_ref[...] = _rms(x1, g_ref[...]).astype(BF16)
    acc_ref[...] = jnp.zeros(acc_ref.shape, F32)

    def up(j, ubuf):
        for s in range(2):
            lo = s * D_FF + j * FFN_CHUNK
            u = jnp.dot(h_ref[...], wup_ref[:, lo:lo + FFN_CHUNK], preferred_element_type=F32)
            for c in range(halves):
                ubuf[s, c, SUBLANES:, :] = u[:, c * LANES:(c + 1) * LANES]

    def gate_down(j, ubuf):
        ys = []
        for s in range(2):
            outs = []
            for c in range(halves):
                lo = s * D_FF + j * FFN_CHUNK + c * LANES
                cols = slice(lo, lo + LANES)
                hist = carry_ref.at[s * nch + j, :, c * LANES:(c + 1) * LANES]
                ubuf[s, c, 0:SUBLANES, :] = hist[...]
                y = cb_ref[:, cols]
                for d in range(taps):
                    y = y + cw_ref[taps - 1 - d:taps - d, cols] * ubuf[s, c, pl.ds(SUBLANES - d, tm), :]
                hist[...] = ubuf[s, c, tm:tm + SUBLANES, :]
                outs.append(y)
            ys.append(jnp.concatenate(outs, axis=1))
        val, gate = ys
        a = (val * _sigmoid(val) * gate).astype(BF16)
        if j % 2 == 0 and j + 1 < nch:
            abuf_ref[:, 0:FFN_CHUNK] = a
        elif j % 2 == 1:
            abuf_ref[:, FFN_CHUNK:] = a
            acc_ref[...] += jnp.dot(abuf_ref[...], wdn_ref[(j - 1) * FFN_CHUNK:(j + 1) * FFN_CHUNK, :],
                                    preferred_element_type=F32)
        else:
            acc_ref[...] += jnp.dot(a, wdn_ref[j * FFN_CHUNK:(j + 1) * FFN_CHUNK, :],
                                    preferred_element_type=F32)

    assert nch % 2 == 1
    up(0, ubuf_a)
    for j in range(0, nch - 1, 2):
        up(j + 1, ubuf_b)
        gate_down(j, ubuf_a)
        up(j + 2, ubuf_a)
        gate_down(j + 1, ubuf_b)
    gate_down(nch - 1, ubuf_a)
    y = out_ref[...] + acc_ref[...]
    if final_norm:
        y = _rms(y, fg_ref[...])
    out_ref[...] = y


def _ffn(x2, ya, yb, yc, wo, g, wup, cw, cb, wdn, fg, layer, tm, seq, final_norm):
    t = x2.shape[0]
    row = lambda n: pl.BlockSpec((tm, n), lambda i: (i, 0))
    nch2 = 2 * D_FF // FFN_CHUNK
    body = functools.partial(_ffn_body, tiles_per_seq=seq // tm, final_norm=final_norm)
    return pl.pallas_call(
        body,
        grid=(t // tm,),
        in_specs=([row(D_MODEL), row(CONV_CH), row(MOBA_WIDTH), row(GLA_WIDTH)]
                  + [_resident_layer(p, layer) for p in (wo, g, wup, cw, cb, wdn)]
                  + [_resident_layer(fg, 0)]),
        out_specs=row(D_MODEL),
        out_shape=jax.ShapeDtypeStruct((t, D_MODEL), F32),
        scratch_shapes=[pltpu.VMEM((tm, D_MODEL), BF16),
                        pltpu.VMEM((tm, D_MODEL), F32),
                        pltpu.VMEM((2, FFN_CHUNK // LANES, SUBLANES + tm, LANES), F32),
                        pltpu.VMEM((2, FFN_CHUNK // LANES, SUBLANES + tm, LANES), F32),
                        pltpu.VMEM((nch2, SUBLANES, FFN_CHUNK), F32),
                        pltpu.VMEM((tm, 2 * FFN_CHUNK), BF16)],
        compiler_params=_params(),
        name="ffn",
    )(x2, ya, yb, yc, wo, g, wup, cw, cb, wdn, fg)


def _pack_w_in_body(win_t_ref, win_o):
    o_g = _C_R
    o_r = o_g + GLA_GATE_RANK
    win_o[0:o_g, :] = win_t_ref[0:o_g, :].astype(BF16)
    win_o[_C_R:_C_G, :] = win_t_ref[o_r:, :].astype(BF16)
    win_o[_C_G:_C_G + GLA_GATE_RANK, :] = win_t_ref[o_g:o_r, :].astype(BF16)
    win_o[_C_G + GLA_GATE_RANK:, :] = jnp.zeros(
        (_IN_COLS_PAD - _C_G - GLA_GATE_RANK, win_o.shape[1]), BF16)


def _pack_w_in(w_in_t):
    depth, cols, d = w_in_t.shape
    whole = lambda rows: pl.BlockSpec((None, rows, d), lambda l: (l, 0, 0))
    return pl.pallas_call(
        _pack_w_in_body,
        grid=(depth,),
        in_specs=[whole(cols)],
        out_specs=whole(_IN_COLS_PAD),
        out_shape=jax.ShapeDtypeStruct((depth, _IN_COLS_PAD, d), BF16),
        compiler_params=_params(),
        name="w_in_cast",
    )(w_in_t)


def kernel(x, norm_mix_g, w_in, conv_w, conv_b, conv_ln_g, conv_ln_b, moba_out_g,
           gla_gate_w, gla_gate_b, gla_out_g, w_out, norm_ffn_g, ffn_w_up, ffn_conv_w,
           ffn_conv_b, ffn_w_down, final_g):
    bsz, seq, d = x.shape
    depth = w_in.shape[0]
    x2 = x.reshape(bsz * seq, d)
    rows = lambda v: v.reshape(v.shape[0], 1, -1)
    w_in_p = _pack_w_in(jnp.swapaxes(w_in, 1, 2))
    for l in range(depth):
        res = _in_proj(
            x2, rows(norm_mix_g), w_in_p, l, gla_gate_w, rows(gla_gate_b), conv_w, rows(conv_b),
            rows(conv_ln_g), rows(conv_ln_b), _IN_PROJ_TILE, seq,
            cast=(w_out, ffn_w_up, ffn_w_down) if l == 0 else ())
        ya, qkv, gq, gk, gv, gr, la = res[:_N_IN_PROJ_OUTS]
        if l == 0:
            w_out_p, w_up_p, w_dn_p = res[_N_IN_PROJ_OUTS:]
        yb = _moba(qkv, rows(moba_out_g), l, seq)
        yc = _gla(gq, gk, gv, la, gr, rows(gla_out_g), l, seq)
        x2 = _ffn(x2, ya, yb, yc, w_out_p, rows(norm_ffn_g), w_up_p, ffn_conv_w,
                  rows(ffn_conv_b), w_dn_p, rows(final_g[None]), l, _FFN_TILE, seq,
                  final_norm=(l == depth - 1))
    return x2.reshape(bsz, seq, d)
```

```python
import functools

import jax
import jax.numpy as jnp
from jax import lax
from jax.experimental import pallas as pl
from jax.experimental.pallas import tpu as pltpu

F32 = jnp.float32
BF16 = jnp.bfloat16

D_MODEL = 1024
CONV_CH = 256
CONV_WIDTH = 31
MOBA_HEADS = 4
MOBA_DH = 64
MOBA_WIDTH = 256
MOBA_BLOCK = 256
MOBA_TOPK = 3
GLA_HEADS = 4
GLA_DK = 64
GLA_DV = 128
GLA_WIDTH = 512
GLA_GATE_RANK = 16
GLA_TAU = 16.0
GLA_BLOCK = 256
GLA_SUB = 32
D_FF = 2816
FFN_CHUNK = 256
FFN_CONV_WIDTH = 3
NORM_EPS = 1e-6
LN_EPS = 1e-5
NEG_INF = -1e30
LANES = 128
SUBLANES = 8
V7X_VMEM_BYTES = 64 * 1024 * 1024
VMEM_LIMIT = V7X_VMEM_BYTES - 8 * 1024 * 1024

_C_A = 0
_C_B = _C_A + 2 * CONV_CH
_C_Q = _C_B + 3 * MOBA_WIDTH
_C_K = _C_Q + GLA_HEADS * GLA_DK
_C_V = _C_K + GLA_HEADS * GLA_DK
_C_R = _C_V + GLA_WIDTH
_C_G = _C_R + GLA_WIDTH
_IN_COLS_PAD = _C_G + LANES


def _sigmoid(x):
    return 1.0 / (1.0 + jnp.exp(-x))


def _rms(x, g):
    return x * lax.rsqrt(jnp.mean(x * x, axis=-1, keepdims=True) + NORM_EPS) * g


def _nt(a, b):
    return lax.dot_general(a, b, (((1,), (1,)), ((), ())), preferred_element_type=F32)


def _params(n_axes=1):
    return pltpu.CompilerParams(dimension_semantics=("arbitrary",) * n_axes,
                                vmem_limit_bytes=VMEM_LIMIT)


def _row_views(body, positions, layer):
    def wrapped(*refs):
        refs = list(refs)
        for p in positions:
            refs[p] = refs[p].at[layer:layer + 1, :]
        return body(*refs)
    return wrapped


def _resident_layer(stacked, layer):
    if stacked.ndim == 2:
        return pl.BlockSpec(stacked.shape, lambda *_: (0, 0), pipeline_mode=pl.Buffered(1))
    tail = stacked.shape[1:]
    return pl.BlockSpec((None,) + tail, lambda *_: (layer,) + (0,) * len(tail),
                        pipeline_mode=pl.Buffered(1))


_IN_PROJ_TILE = 1024
_IN_PROJ_SUB = 512
_CONV_PAD = 32
_CONV_ROWS = 64


_N_IN_PROJ_INS = 9
_N_IN_PROJ_OUTS = 7


def _in_proj_body(*refs, tiles_per_seq, n_cast):
    x_ref, g_ref, w_ref, gw_ref, gb_ref, cw_ref, cb_ref, lg_ref, lb_ref = refs[:_N_IN_PROJ_INS]
    outs = refs[_N_IN_PROJ_INS + n_cast:]
    ya_ref, qkv_ref, gq_ref, gk_ref, gv_ref, gr_ref, la_ref = outs[:_N_IN_PROJ_OUTS]
    hp_ref = outs[_N_IN_PROJ_OUTS + n_cast]
    for src, dst in zip(refs[_N_IN_PROJ_INS:_N_IN_PROJ_INS + n_cast],
                        outs[_N_IN_PROJ_OUTS:_N_IN_PROJ_OUTS + n_cast]):
        dst[...] = src[...].astype(BF16)
    tm = x_ref.shape[0]
    halves = CONV_CH // LANES
    first = pl.program_id(0) % tiles_per_seq == 0

    @pl.when(first)
    def _():
        hp_ref[:, 0:_CONV_PAD, :] = jnp.zeros((halves, _CONV_PAD, LANES), F32)

    @pl.when(jnp.logical_not(first))
    def _():
        hp_ref[:, 0:_CONV_PAD, :] = hp_ref[:, tm:tm + _CONV_PAD, :]

    gate_w = jnp.concatenate(
        [gw_ref[...], jnp.zeros((LANES - GLA_GATE_RANK, gw_ref.shape[1]), F32)], axis=0).astype(BF16)

    def norm(rows):
        return _rms(x_ref[rows, :], g_ref[...]).astype(BF16)

    def conformer(r0):
        accs = []
        for c in range(halves):
            cols = slice(c * LANES, (c + 1) * LANES)
            acc = jnp.broadcast_to(cb_ref[:, cols], (_CONV_ROWS, LANES))
            for i in range(CONV_WIDTH):
                off = r0 + _CONV_PAD - (CONV_WIDTH - 1) + i
                acc = acc + cw_ref[i:i + 1, cols] * hp_ref[c, off:off + _CONV_ROWS, :]
            accs.append(acc)
        y = jnp.concatenate(accs, axis=1)
        d = y - jnp.mean(y, axis=-1, keepdims=True)
        var = jnp.mean(d * d, axis=-1, keepdims=True)
        y = d * lax.rsqrt(var + LN_EPS) * lg_ref[...] + lb_ref[...]
        ya_ref[r0:r0 + _CONV_ROWS, :] = (y * _sigmoid(y)).astype(ya_ref.dtype)

    def project(rows, h):
        def mm(lo, hi):
            return _nt(h, w_ref[lo:hi, :])

        a = mm(_C_A, _C_B)
        glu = a[:, :CONV_CH] * _sigmoid(a[:, CONV_CH:])
        for c in range(halves):
            hp_ref[c, _CONV_PAD + rows.start:_CONV_PAD + rows.stop, :] = glu[:, c * LANES:(c + 1) * LANES]
        def moba_qkv():
            qkv_ref[rows, :] = mm(_C_B, _C_Q).astype(BF16)

        def gla_q():
            gq_ref[rows, :] = mm(_C_Q, _C_K) * (GLA_DK ** -0.5)

        def gla_k():
            gk_ref[rows, :] = mm(_C_K, _C_V)

        def gla_v():
            gv_ref[rows, :] = mm(_C_V, _C_R).astype(BF16)

        def gla_r():
            r = mm(_C_R, _C_G)
            gr_ref[rows, :] = (r * _sigmoid(r)).astype(BF16)

        def gla_gate():
            z = jnp.dot(mm(_C_G, _IN_COLS_PAD).astype(BF16), gate_w,
                        preferred_element_type=F32) + gb_ref[...]
            la_ref[rows, :] = (jnp.minimum(z, 0.0) - jnp.log1p(jnp.exp(-jnp.abs(z)))) * (1.0 / GLA_TAU)

        groups = [moba_qkv, gla_q, gla_k, gla_v, gla_r, gla_gate]
        passes = list(range(rows.start, rows.stop, _CONV_ROWS))
        per = -(-len(passes) // len(groups))
        for gi, group in enumerate(groups):
            group()
            for r0 in passes[gi * per:(gi + 1) * per]:
                conformer(r0)

    parts = [slice(r0, r0 + _IN_PROJ_SUB) for r0 in range(0, tm, _IN_PROJ_SUB)]
    h_next = norm(parts[0])
    for k, rows in enumerate(parts):
        h = h_next
        if k + 1 < len(parts):
            h_next = norm(parts[k + 1])
        project(rows, h)


def _in_proj(x2, g, w, layer, gw, gb, cw, cb, lg, lb, tm, seq, cast=()):
    t = x2.shape[0]
    steps = t // tm
    row = lambda n: pl.BlockSpec((tm, n), lambda i: (i, 0))
    outs = [(CONV_CH, BF16), (3 * MOBA_WIDTH, BF16), (GLA_HEADS * GLA_DK, F32), (GLA_HEADS * GLA_DK, F32),
            (GLA_WIDTH, BF16), (GLA_WIDTH, BF16), (GLA_HEADS * GLA_DK, F32)]

    def slab(arr):
        per_layer = steps // arr.shape[0]
        assert per_layer * arr.shape[0] == steps and arr.shape[1] % (per_layer * 2 * SUBLANES) == 0
        return pl.BlockSpec((None, arr.shape[1] // per_layer, arr.shape[2]),
                            lambda i: (i // per_layer, i % per_layer, 0))

    return pl.pallas_call(
        _row_views(functools.partial(_in_proj_body, tiles_per_seq=seq // tm, n_cast=len(cast)),
                   (1, 4, 6, 7, 8), layer),
        grid=(steps,),
        in_specs=([row(D_MODEL)] + [_resident_layer(p, layer) for p in (g, w, gw, gb, cw, cb, lg, lb)]
                  + [slab(a) for a in cast]),
        out_specs=[row(n) for n, _ in outs] + [slab(a) for a in cast],
        out_shape=([jax.ShapeDtypeStruct((t, n), d) for n, d in outs]
                   + [jax.ShapeDtypeStruct(a.shape, BF16) for a in cast]),
        scratch_shapes=[pltpu.VMEM((CONV_CH // LANES, _CONV_PAD + tm, LANES), F32)],
        compiler_params=_params(),
        name="in_proj",
    )(x2, g, w, gw, gb, cw, cb, lg, lb, *cast)


_LOG2E = 1.4426950408889634


def _moba_body(q_ref, k_ref, v_ref, g_ref, out_ref, kaug_ref, qaug_t_ref, vaug_t_ref,
               s_a_ref, s_b_ref, e_a_ref, e_b_ref):
    seq = q_ref.shape[0]
    nb = seq // MOBA_BLOCK
    blk = MOBA_BLOCK
    dh = MOBA_DH
    aug = 2 * dh

    row_blk = lax.broadcasted_iota(jnp.int32, (seq, dh), 0) // blk
    lane = lax.broadcasted_iota(jnp.int32, (seq, dh), 1)
    onehot = jnp.where(lane == row_blk, 1.0, 0.0).astype(BF16)
    n_idx = lax.broadcasted_iota(jnp.int32, (nb, seq), 0)
    own = lax.broadcasted_iota(jnp.int32, (nb, seq), 1) // blk
    ones_row = jnp.where(lax.broadcasted_iota(jnp.int32, (SUBLANES, seq), 0) == 0, 1.0, 0.0)
    q_t = q_ref[...].astype(F32).T
    v_t = v_ref[...].astype(F32).T
    for h in range(MOBA_HEADS):
        hs = slice(h * dh, (h + 1) * dh)
        kh = k_ref[:, hs]
        kaug_ref[h] = jnp.concatenate([kh, onehot], axis=1)
        km = jnp.sum(kh.astype(F32).reshape(nb, blk, dh), axis=1) * (1.0 / blk)
        km_hi = km.astype(BF16)
        km_lo = (km - km_hi.astype(F32)).astype(BF16)
        qh_t = q_t[hs, :]
        qh_t16 = qh_t.astype(BF16)
        gate = (jnp.dot(km_hi, qh_t16, preferred_element_type=F32)
                + jnp.dot(km_lo, qh_t16, preferred_element_type=F32))
        gm = jnp.where(n_idx < own, gate, NEG_INF)
        rank = jnp.zeros((nb, seq), F32)
        for m in range(nb):
            gm_m = gm[m:m + 1, :]
            beats = (gm_m > gm) | ((gm_m == gm) & (m < n_idx))
            rank = rank + jnp.where(beats, 1.0, 0.0)
        keep = ((rank < MOBA_TOPK) & (n_idx < own)) | (n_idx >= own)
        bias = jnp.where(keep, 0.0, NEG_INF)
        qaug_t_ref[h] = jnp.concatenate(
            [qh_t * (dh ** -0.5 * _LOG2E), bias, jnp.zeros((aug - dh - nb, seq), F32)], axis=0).astype(BF16)
        vaug_t_ref[h] = jnp.concatenate(
            [v_t[hs, :], ones_row, jnp.zeros((aug - dh - SUBLANES, seq), F32)], axis=0).astype(BF16)

    causal = (lax.broadcasted_iota(jnp.int32, (blk, blk), 0)
              <= lax.broadcasted_iota(jnp.int32, (blk, blk), 1))

    def scores(i, h, s_ref):
        cur = slice(i * blk, (i + 1) * blk)
        qa = qaug_t_ref[h, :, cur]
        s_own = jnp.dot(kaug_ref[h, cur, :], qa, preferred_element_type=F32)
        s_own = jnp.where(causal, s_own, NEG_INF)
        s_ref[cur, :] = s_own
        m_col = jnp.max(s_own, axis=0, keepdims=True)
        if i > 0:
            s_past = jnp.dot(kaug_ref[h, 0:i * blk, :], qa, preferred_element_type=F32)
            s_ref[0:i * blk, :] = s_past
            m_col = jnp.maximum(m_col, jnp.max(s_past, axis=0, keepdims=True))
        return m_col

    def weights(i, s_ref, m_col, e_ref):
        keys = slice(0, (i + 1) * blk)
        e_ref[keys, :] = jnp.exp2(s_ref[keys, :] - m_col).astype(BF16)

    def finish(i, h, e_ref):
        keys = slice(0, (i + 1) * blk)
        o = jnp.dot(vaug_t_ref[h, :, keys], e_ref[keys, :], preferred_element_type=F32)
        o = o[0:dh, :] / o[dh:dh + 1, :]
        return o * lax.rsqrt(jnp.mean(o * o, axis=0, keepdims=True) + NORM_EPS)

    units = [(i, h) for i in range(nb) for h in range(MOBA_HEADS)]
    s_bufs = (s_a_ref, s_b_ref)
    e_bufs = (e_a_ref, e_b_ref)
    m_cols = {0: scores(*units[0], s_bufs[0]), 1: scores(*units[1], s_bufs[1])}
    weights(units[0][0], s_bufs[0], m_cols.pop(0), e_bufs[0])
    outs = []
    for u, (i, h) in enumerate(units):
        if u + 2 < len(units):
            m_cols[u + 2] = scores(*units[u + 2], s_bufs[u % 2])
        if u + 1 < len(units):
            weights(units[u + 1][0], s_bufs[(u + 1) % 2], m_cols.pop(u + 1), e_bufs[(u + 1) % 2])
        outs.append(finish(i, h, e_bufs[u % 2]))
        if h == MOBA_HEADS - 1:
            y = jnp.concatenate(outs, axis=0).T
            out_ref[i * blk:(i + 1) * blk, :] = (y * g_ref[...]).astype(out_ref.dtype)
            outs = []


def _moba(qkv, g, layer, seq):
    t = qkv.shape[0]
    col = lambda c: pl.BlockSpec((seq, MOBA_WIDTH), lambda b: (b, c))
    return pl.pallas_call(
        _row_views(_moba_body, (3,), layer),
        grid=(t // seq,),
        in_specs=[col(0), col(1), col(2), _resident_layer(g, layer)],
        out_specs=pl.BlockSpec((seq, MOBA_WIDTH), lambda b: (b, 0)),
        out_shape=jax.ShapeDtypeStruct((t, MOBA_WIDTH), BF16),
        scratch_shapes=[pltpu.VMEM((MOBA_HEADS, seq, 2 * MOBA_DH), BF16),
                        pltpu.VMEM((MOBA_HEADS, 2 * MOBA_DH, seq), BF16),
                        pltpu.VMEM((MOBA_HEADS, 2 * MOBA_DH, seq), BF16),
                        pltpu.VMEM((seq, MOBA_BLOCK), F32),
                        pltpu.VMEM((seq, MOBA_BLOCK), F32),
                        pltpu.VMEM((seq, MOBA_BLOCK), BF16),
                        pltpu.VMEM((seq, MOBA_BLOCK), BF16)],
        compiler_params=_params(),
        name="moba",
    )(qkv, qkv, qkv, g)


def _split3(x):
    hi = x.astype(BF16)
    r = x - hi.astype(F32)
    mid = r.astype(BF16)
    lo = (r - mid.astype(F32)).astype(BF16)
    return hi, mid, lo


def _gla_body(q_ref, k_ref, v_ref, la_ref, r_ref, g_ref, out_ref, st_ref):
    blk = GLA_BLOCK
    sub = GLA_SUB
    nsub = blk // sub
    dk = GLA_DK
    dv = GLA_DV

    @pl.when(pl.program_id(1) == 0)
    def _():
        st_ref[...] = jnp.zeros(st_ref.shape, F32)

    nblk = q_ref.shape[0] // blk
    ri = lax.broadcasted_iota(jnp.int32, (blk, blk), 0)
    ci = lax.broadcasted_iota(jnp.int32, (blk, blk), 1)
    tril = jnp.where(ci <= ri, 1.0, 0.0).astype(BF16)
    causal = ci <= ri
    b_all = [sum(jnp.dot(tril, part, preferred_element_type=F32)
                 for part in _split3(la_ref[n * blk:(n + 1) * blk, :])) for n in range(nblk)]

    pair = LANES // dk
    lane_head = lax.broadcasted_iota(jnp.int32, (1, LANES), 1) // dk

    def prep(n, p):
        ls = slice(p * LANES, (p + 1) * LANES)
        rows = slice(n * blk, (n + 1) * blk)
        q = q_ref[rows, ls]
        k = k_ref[rows, ls]
        b = b_all[n][:, ls]
        s = [jnp.zeros((1, LANES), F32)] + [b[I * sub - 1:I * sub, :] for I in range(1, nsub)]
        s_row = jnp.concatenate([jnp.broadcast_to(si, (sub, LANES)) for si in s], axis=0)
        b_end = b[blk - 1:blk, :]
        qt = (q * jnp.exp(b - s_row)).astype(BF16)
        q_in = (q * jnp.exp(b)).astype(BF16)
        k_out = (k * jnp.exp(b_end - b)).astype(BF16)
        k_sub = []
        for I in range(nsub):
            nk = (I + 1) * sub
            k_i = (k[0:nk, :] * jnp.exp(s[I] - b[0:nk, :])).astype(BF16)
            if nk < blk:
                k_i = jnp.concatenate([k_i, jnp.zeros((blk - nk, LANES), BF16)], axis=0)
            k_sub.append(k_i)
        heads = []
        for j in range(pair):
            mine = lane_head == j
            qt_h = jnp.where(mine, qt, 0.0)
            a = jnp.concatenate([_nt(qt_h[I * sub:(I + 1) * sub, :], k_sub[I]) for I in range(nsub)], axis=0)
            a = jnp.where(causal, a, 0.0).astype(BF16)
            heads.append((a, jnp.where(mine, q_in, 0.0), mine))
        return heads, k_out, jnp.exp(b_end)

    def finish(n, h, a, q_in, mine, k_out, decay):
        vs = slice(h * dv, (h + 1) * dv)
        rows = slice(n * blk, (n + 1) * blk)
        v = v_ref[rows, vs]
        st = st_ref[h]
        o = jnp.dot(a, v, preferred_element_type=F32) + _nt(q_in, st.astype(BF16))
        v_t = v.astype(F32).T.astype(BF16)
        st_ref[h] = jnp.where(mine, st * decay + jnp.dot(v_t, k_out, preferred_element_type=F32), 0.0)
        o = o * lax.rsqrt(jnp.mean(o * o, axis=-1, keepdims=True) + NORM_EPS) * g_ref[:, vs]
        out_ref[rows, vs] = (o * r_ref[rows, vs].astype(F32)).astype(out_ref.dtype)

    units = [(n, p) for n in range(nblk) for p in range(GLA_HEADS // pair)]
    nxt = prep(*units[0])
    for u, (n, p) in enumerate(units):
        heads, k_out, decay = nxt
        if u + 1 < len(units):
            nxt = prep(*units[u + 1])
        for j, (a, q_in, mine) in enumerate(heads):
            finish(n, p * pair + j, a, q_in, mine, k_out, decay)


_GLA_STEP_BLOCKS = 4


def _gla(gq, gk, gv, la, gr, g, layer, seq):
    t = gq.shape[0]
    rows = GLA_BLOCK * _GLA_STEP_BLOCKS
    nblk = seq // rows
    blk = lambda n: pl.BlockSpec((rows, n), lambda b, i: (b * nblk + i, 0))
    kw = GLA_HEADS * GLA_DK
    return pl.pallas_call(
        _row_views(_gla_body, (5,), layer),
        grid=(t // seq, nblk),
        in_specs=[blk(kw), blk(kw), blk(GLA_WIDTH), blk(kw), blk(GLA_WIDTH), _resident_layer(g, layer)],
        out_specs=blk(GLA_WIDTH),
        out_shape=jax.ShapeDtypeStruct((t, GLA_WIDTH), BF16),
        scratch_shapes=[pltpu.VMEM((GLA_HEADS, GLA_DV, LANES), F32)],
        compiler_params=_params(2),
        name="gla",
    )(gq, gk, gv, la, gr, g)


_FFN_TILE = 512


def _ffn_body(x_ref, ya_ref, yb_ref, yc_ref, wo_ref, g_ref, wup_ref, cw_ref, cb_ref, wdn_ref,
              fg_ref, out_ref, h_ref, acc_ref, ubuf_a, ubuf_b, carry_ref, abuf_ref,
              *, tiles_per_seq, final_norm):
    tm = x_ref.shape[0]
    nch = D_FF // FFN_CHUNK
    halves = FFN_CHUNK // LANES
    taps = FFN_CONV_WIDTH
    o_b = CONV_CH
    o_c = CONV_CH + MOBA_WIDTH

    @pl.when(pl.program_id(0) % tiles_per_seq == 0)
    def _():
        carry_ref[...] = jnp.zeros(carry_ref.shape, F32)

    y = jnp.dot(ya_ref[...], wo_ref[0:o_b, :], preferred_element_type=F32)
    y = y + jnp.dot(yb_ref[...], wo_ref[o_b:o_c, :], preferred_element_type=F32)
    y = y + jnp.dot(yc_ref[...], wo_ref[o_c:, :], preferred_element_type=F32)
    x1 = x_ref[...] + y
    out_ref[...] = x1
    h_ref[...] = _rms(x1, g_ref[...]).astype(BF16)
    acc_ref[...] = jnp.zeros(acc_ref.shape, F32)

    def up(j, ubuf):
        for s in range(2):
            lo = s * D_FF + j * FFN_CHUNK
            u = jnp.dot(h_ref[...], wup_ref[:, lo:lo + FFN_CHUNK], preferred_element_type=F32)
            for c in range(halves):
                ubuf[s, c, SUBLANES:, :] = u[:, c * LANES:(c + 1) * LANES]

    def gate_down(j, ubuf):
        ys = []
        for s in range(2):
            outs = []
            for c in range(halves):
                lo = s * D_FF + j * FFN_CHUNK + c * LANES
                cols = slice(lo, lo + LANES)
                hist = carry_ref.at[s * nch + j, :, c * LANES:(c + 1) * LANES]
                ubuf[s, c, 0:SUBLANES, :] = hist[...]
                y = cb_ref[:, cols]
                for d in range(taps):
                    y = y + cw_ref[taps - 1 - d:taps - d, cols] * ubuf[s, c, pl.ds(SUBLANES - d, tm), :]
                hist[...] = ubuf[s, c, tm:tm + SUBLANES, :]
                outs.append(y)
            ys.append(jnp.concatenate(outs, axis=1))
        val, gate = ys
        a = (val * _sigmoid(val) * gate).astype(BF16)
        if j % 2 == 0 and j + 1 < nch:
            abuf_ref[:, 0:FFN_CHUNK] = a
        elif j % 2 == 1:
            abuf_ref[:, FFN_CHUNK:] = a
            acc_ref[...] += jnp.dot(abuf_ref[...], wdn_ref[(j - 1) * FFN_CHUNK:(j + 1) * FFN_CHUNK, :],
                                    preferred_element_type=F32)
        else:
            acc_ref[...] += jnp.dot(a, wdn_ref[j * FFN_CHUNK:(j + 1) * FFN_CHUNK, :],
                                    preferred_element_type=F32)

    assert nch % 2 == 1
    up(0, ubuf_a)
    for j in range(0, nch - 1, 2):
        up(j + 1, ubuf_b)
        gate_down(j, ubuf_a)
        up(j + 2, ubuf_a)
        gate_down(j + 1, ubuf_b)
    gate_down(nch - 1, ubuf_a)
    y = out_ref[...] + acc_ref[...]
    if final_norm:
        y = _rms(y, fg_ref[...])
    out_ref[...] = y


def _ffn(x2, ya, yb, yc, wo, g, wup, cw, cb, wdn, fg, layer, tm, seq, final_norm):
    t = x2.shape[0]
    row = lambda n: pl.BlockSpec((tm, n), lambda i: (i, 0))
    nch2 = 2 * D_FF // FFN_CHUNK
    body = functools.partial(_ffn_body, tiles_per_seq=seq // tm, final_norm=final_norm)
    body = _row_views(_row_views(body, (10,), 0), (5, 8), layer)
    return pl.pallas_call(
        body,
        grid=(t // tm,),
        in_specs=([row(D_MODEL), row(CONV_CH), row(MOBA_WIDTH), row(GLA_WIDTH)]
                  + [_resident_layer(p, layer) for p in (wo, g, wup, cw, cb, wdn)]
                  + [_resident_layer(fg, 0)]),
        out_specs=row(D_MODEL),
        out_shape=jax.ShapeDtypeStruct((t, D_MODEL), F32),
        scratch_shapes=[pltpu.VMEM((tm, D_MODEL), BF16),
                        pltpu.VMEM((tm, D_MODEL), F32),
                        pltpu.VMEM((2, FFN_CHUNK // LANES, SUBLANES + tm, LANES), F32),
                        pltpu.VMEM((2, FFN_CHUNK // LANES, SUBLANES + tm, LANES), F32),
                        pltpu.VMEM((nch2, SUBLANES, FFN_CHUNK), F32),
                        pltpu.VMEM((tm, 2 * FFN_CHUNK), BF16)],
        compiler_params=_params(),
        name="ffn",
    )(x2, ya, yb, yc, wo, g, wup, cw, cb, wdn, fg)


def _pack_w_in_body(win_t_ref, win_o):
    o_g = _C_R
    o_r = o_g + GLA_GATE_RANK
    win_o[0:o_g, :] = win_t_ref[0:o_g, :].astype(BF16)
    win_o[_C_R:_C_G, :] = win_t_ref[o_r:, :].astype(BF16)
    win_o[_C_G:_C_G + GLA_GATE_RANK, :] = win_t_ref[o_g:o_r, :].astype(BF16)
    win_o[_C_G + GLA_GATE_RANK:, :] = jnp.zeros(
        (_IN_COLS_PAD - _C_G - GLA_GATE_RANK, win_o.shape[1]), BF16)


def _pack_w_in(w_in_t):
    depth, cols, d = w_in_t.shape
    whole = lambda rows: pl.BlockSpec((None, rows, d), lambda l: (l, 0, 0))
    return pl.pallas_call(
        _pack_w_in_body,
        grid=(depth,),
        in_specs=[whole(cols)],
        out_specs=whole(_IN_COLS_PAD),
        out_shape=jax.ShapeDtypeStruct((depth, _IN_COLS_PAD, d), BF16),
        compiler_params=_params(),
        name="w_in_cast",
    )(w_in_t)


def kernel(x, norm_mix_g, w_in, conv_w, conv_b, conv_ln_g, conv_ln_b, moba_out_g,
           gla_gate_w, gla_gate_b, gla_out_g, w_out, norm_ffn_g, ffn_w_up, ffn_conv_w,
           ffn_conv_b, ffn_w_down, final_g):
    bsz, seq, d = x.shape
    depth = w_in.shape[0]
    x2 = x.reshape(bsz * seq, d)
    w_in_p = _pack_w_in(jnp.swapaxes(w_in, 1, 2))
    for l in range(depth):
        res = _in_proj(
            x2, norm_mix_g, w_in_p, l, gla_gate_w, gla_gate_b, conv_w, conv_b,
            conv_ln_g, conv_ln_b, _IN_PROJ_TILE, seq,
            cast=(w_out, ffn_w_up, ffn_w_down) if l == 0 else ())
        ya, qkv, gq, gk, gv, gr, la = res[:_N_IN_PROJ_OUTS]
        if l == 0:
            w_out_p, w_up_p, w_dn_p = res[_N_IN_PROJ_OUTS:]
        yb = _moba(qkv, moba_out_g, l, seq)
        yc = _gla(gq, gk, gv, la, gr, gla_out_g, l, seq)
        x2 = _ffn(x2, ya, yb, yc, w_out_p, norm_ffn_g, w_up_p, ffn_conv_w,
                  ffn_conv_b, w_dn_p, final_g[None], l, _FFN_TILE, seq,
                  final_norm=(l == depth - 1))
    return x2.reshape(bsz, seq, d)
```

```python
import functools

import jax
import jax.numpy as jnp
from jax import lax
from jax.experimental import pallas as pl
from jax.experimental.pallas import tpu as pltpu

F32 = jnp.float32
BF16 = jnp.bfloat16

D_MODEL = 1024
CONV_CH = 256
CONV_WIDTH = 31
MOBA_HEADS = 4
MOBA_DH = 64
MOBA_WIDTH = 256
MOBA_BLOCK = 256
MOBA_TOPK = 3
GLA_HEADS = 4
GLA_DK = 64
GLA_DV = 128
GLA_WIDTH = 512
GLA_GATE_RANK = 16
GLA_TAU = 16.0
GLA_BLOCK = 256
GLA_SUB = 32
D_FF = 2816
FFN_CHUNK = 256
FFN_CONV_WIDTH = 3
NORM_EPS = 1e-6
LN_EPS = 1e-5
NEG_INF = -1e30
LANES = 128
SUBLANES = 8
V7X_VMEM_BYTES = 64 * 1024 * 1024
VMEM_LIMIT = V7X_VMEM_BYTES - 8 * 1024 * 1024

_C_A = 0
_C_B = _C_A + 2 * CONV_CH
_C_Q = _C_B + 3 * MOBA_WIDTH
_C_K = _C_Q + GLA_HEADS * GLA_DK
_C_V = _C_K + GLA_HEADS * GLA_DK
_C_R = _C_V + GLA_WIDTH
_C_G = _C_R + GLA_WIDTH
_IN_COLS_PAD = _C_G + LANES


def _sigmoid(x):
    return 1.0 / (1.0 + jnp.exp(-x))


def _rms(x, g):
    return x * lax.rsqrt(jnp.mean(x * x, axis=-1, keepdims=True) + NORM_EPS) * g


def _nt(a, b):
    return lax.dot_general(a, b, (((1,), (1,)), ((), ())), preferred_element_type=F32)


def _params(n_axes=1):
    return pltpu.CompilerParams(dimension_semantics=("arbitrary",) * n_axes,
                                vmem_limit_bytes=VMEM_LIMIT)


def _row_views(body, positions, layer):
    def wrapped(*refs):
        refs = list(refs)
        for p in positions:
            refs[p] = refs[p].at[layer:layer + 1, :]
        return body(*refs)
    return wrapped


def _resident_layer(stacked, layer):
    if stacked.ndim == 2:
        return pl.BlockSpec(stacked.shape, lambda *_: (0, 0), pipeline_mode=pl.Buffered(1))
    tail = stacked.shape[1:]
    return pl.BlockSpec((None,) + tail, lambda *_: (layer,) + (0,) * len(tail),
                        pipeline_mode=pl.Buffered(1))


_IN_PROJ_TILE = 1024
_IN_PROJ_SUB = 512
_CONV_PAD = 32
_CONV_ROWS = 64


_N_IN_PROJ_INS = 9
_N_IN_PROJ_OUTS = 7


def _in_proj_body(*refs, tiles_per_seq, n_cast):
    x_ref, g_ref, w_ref, gw_ref, gb_ref, cw_ref, cb_ref, lg_ref, lb_ref = refs[:_N_IN_PROJ_INS]
    outs = refs[_N_IN_PROJ_INS + n_cast:]
    ya_ref, qkv_ref, gq_ref, gk_ref, gv_ref, gr_ref, la_ref = outs[:_N_IN_PROJ_OUTS]
    hp_ref = outs[_N_IN_PROJ_OUTS + n_cast]
    for src, dst in zip(refs[_N_IN_PROJ_INS:_N_IN_PROJ_INS + n_cast],
                        outs[_N_IN_PROJ_OUTS:_N_IN_PROJ_OUTS + n_cast]):
        dst[...] = src[...].astype(BF16)
    tm = x_ref.shape[0]
    halves = CONV_CH // LANES
    first = pl.program_id(0) % tiles_per_seq == 0

    @pl.when(first)
    def _():
        hp_ref[:, 0:_CONV_PAD, :] = jnp.zeros((halves, _CONV_PAD, LANES), F32)

    @pl.when(jnp.logical_not(first))
    def _():
        hp_ref[:, 0:_CONV_PAD, :] = hp_ref[:, tm:tm + _CONV_PAD, :]

    gate_w = jnp.concatenate(
        [gw_ref[...], jnp.zeros((LANES - GLA_GATE_RANK, gw_ref.shape[1]), F32)], axis=0).astype(BF16)

    def norm(rows):
        return _rms(x_ref[rows, :], g_ref[...]).astype(BF16)

    def conformer(r0):
        accs = []
        for c in range(halves):
            cols = slice(c * LANES, (c + 1) * LANES)
            acc = jnp.broadcast_to(cb_ref[:, cols], (_CONV_ROWS, LANES))
            for i in range(CONV_WIDTH):
                off = r0 + _CONV_PAD - (CONV_WIDTH - 1) + i
                acc = acc + cw_ref[i:i + 1, cols] * hp_ref[c, off:off + _CONV_ROWS, :]
            accs.append(acc)
        y = jnp.concatenate(accs, axis=1)
        d = y - jnp.mean(y, axis=-1, keepdims=True)
        var = jnp.mean(d * d, axis=-1, keepdims=True)
        y = d * lax.rsqrt(var + LN_EPS) * lg_ref[...] + lb_ref[...]
        ya_ref[r0:r0 + _CONV_ROWS, :] = (y * _sigmoid(y)).astype(ya_ref.dtype)

    def project(rows, h):
        def mm(lo, hi):
            return _nt(h, w_ref[lo:hi, :])

        a = mm(_C_A, _C_B)
        glu = a[:, :CONV_CH] * _sigmoid(a[:, CONV_CH:])
        for c in range(halves):
            hp_ref[c, _CONV_PAD + rows.start:_CONV_PAD + rows.stop, :] = glu[:, c * LANES:(c + 1) * LANES]
        for r0 in range(rows.start, rows.stop, _CONV_ROWS):
            conformer(r0)
        qkv_ref[rows, :] = mm(_C_B, _C_Q).astype(BF16)
        gq_ref[rows, :] = mm(_C_Q, _C_K) * (GLA_DK ** -0.5)
        gk_ref[rows, :] = mm(_C_K, _C_V)
        gv_ref[rows, :] = mm(_C_V, _C_R).astype(BF16)
        r = mm(_C_R, _C_G)
        gr_ref[rows, :] = (r * _sigmoid(r)).astype(BF16)
        z = jnp.dot(mm(_C_G, _IN_COLS_PAD).astype(BF16), gate_w,
                    preferred_element_type=F32) + gb_ref[...]
        la_ref[rows, :] = (jnp.minimum(z, 0.0) - jnp.log1p(jnp.exp(-jnp.abs(z)))) * (1.0 / GLA_TAU)

    parts = [slice(r0, r0 + _IN_PROJ_SUB) for r0 in range(0, tm, _IN_PROJ_SUB)]
    h_next = norm(parts[0])
    for k, rows in enumerate(parts):
        h = h_next
        if k + 1 < len(parts):
            h_next = norm(parts[k + 1])
        project(rows, h)


def _in_proj(x2, g, w, layer, gw, gb, cw, cb, lg, lb, tm, seq, cast=()):
    t = x2.shape[0]
    steps = t // tm
    row = lambda n: pl.BlockSpec((tm, n), lambda i: (i, 0))
    outs = [(CONV_CH, BF16), (3 * MOBA_WIDTH, BF16), (GLA_HEADS * GLA_DK, F32), (GLA_HEADS * GLA_DK, F32),
            (GLA_WIDTH, BF16), (GLA_WIDTH, BF16), (GLA_HEADS * GLA_DK, F32)]

    def slab(arr):
        per_layer = steps // arr.shape[0]
        assert per_layer * arr.shape[0] == steps and arr.shape[1] % (per_layer * 2 * SUBLANES) == 0
        return pl.BlockSpec((None, arr.shape[1] // per_layer, arr.shape[2]),
                            lambda i: (i // per_layer, i % per_layer, 0))

    return pl.pallas_call(
        _row_views(functools.partial(_in_proj_body, tiles_per_seq=seq // tm, n_cast=len(cast)),
                   (1, 4, 6, 7, 8), layer),
        grid=(steps,),
        in_specs=([row(D_MODEL)] + [_resident_layer(p, layer) for p in (g, w, gw, gb, cw, cb, lg, lb)]
                  + [slab(a) for a in cast]),
        out_specs=[row(n) for n, _ in outs] + [slab(a) for a in cast],
        out_shape=([jax.ShapeDtypeStruct((t, n), d) for n, d in outs]
                   + [jax.ShapeDtypeStruct(a.shape, BF16) for a in cast]),
        scratch_shapes=[pltpu.VMEM((CONV_CH // LANES, _CONV_PAD + tm, LANES), F32)],
        compiler_params=_params(),
        name="in_proj",
    )(x2, g, w, gw, gb, cw, cb, lg, lb, *cast)


_LOG2E = 1.4426950408889634


def _moba_body(q_ref, k_ref, v_ref, g_ref, out_ref, kaug_ref, qaug_t_ref, vaug_t_ref,
               s_a_ref, s_b_ref, e_a_ref, e_b_ref):
    seq = q_ref.shape[0]
    nb = seq // MOBA_BLOCK
    blk = MOBA_BLOCK
    dh = MOBA_DH
    aug = 2 * dh

    row_blk = lax.broadcasted_iota(jnp.int32, (seq, dh), 0) // blk
    lane = lax.broadcasted_iota(jnp.int32, (seq, dh), 1)
    onehot = jnp.where(lane == row_blk, 1.0, 0.0).astype(BF16)
    n_idx = lax.broadcasted_iota(jnp.int32, (nb, seq), 0)
    own = lax.broadcasted_iota(jnp.int32, (nb, seq), 1) // blk
    ones_row = jnp.where(lax.broadcasted_iota(jnp.int32, (SUBLANES, seq), 0) == 0, 1.0, 0.0)
    q_t = q_ref[...].astype(F32).T
    v_t = v_ref[...].astype(F32).T
    for h in range(MOBA_HEADS):
        hs = slice(h * dh, (h + 1) * dh)
        kh = k_ref[:, hs]
        kaug_ref[h] = jnp.concatenate([kh, onehot], axis=1)
        km = jnp.sum(kh.astype(F32).reshape(nb, blk, dh), axis=1) * (1.0 / blk)
        km_hi = km.astype(BF16)
        km_lo = (km - km_hi.astype(F32)).astype(BF16)
        qh_t = q_t[hs, :]
        qh_t16 = qh_t.astype(BF16)
        gate = (jnp.dot(km_hi, qh_t16, preferred_element_type=F32)
                + jnp.dot(km_lo, qh_t16, preferred_element_type=F32))
        gm = jnp.where(n_idx < own, gate, NEG_INF)
        rank = jnp.zeros((nb, seq), F32)
        for m in range(nb):
            gm_m = gm[m:m + 1, :]
            beats = (gm_m > gm) | ((gm_m == gm) & (m < n_idx))
            rank = rank + jnp.where(beats, 1.0, 0.0)
        keep = ((rank < MOBA_TOPK) & (n_idx < own)) | (n_idx >= own)
        bias = jnp.where(keep, 0.0, NEG_INF)
        qaug_t_ref[h] = jnp.concatenate(
            [qh_t * (dh ** -0.5 * _LOG2E), bias, jnp.zeros((aug - dh - nb, seq), F32)], axis=0).astype(BF16)
        vaug_t_ref[h] = jnp.concatenate(
            [v_t[hs, :], ones_row, jnp.zeros((aug - dh - SUBLANES, seq), F32)], axis=0).astype(BF16)

    causal = (lax.broadcasted_iota(jnp.int32, (blk, blk), 0)
              <= lax.broadcasted_iota(jnp.int32, (blk, blk), 1))

    def scores(i, h, s_ref):
        cur = slice(i * blk, (i + 1) * blk)
        qa = qaug_t_ref[h, :, cur]
        s_own = jnp.dot(kaug_ref[h, cur, :], qa, preferred_element_type=F32)
        s_own = jnp.where(causal, s_own, NEG_INF)
        s_ref[cur, :] = s_own
        m_col = jnp.max(s_own, axis=0, keepdims=True)
        if i > 0:
            s_past = jnp.dot(kaug_ref[h, 0:i * blk, :], qa, preferred_element_type=F32)
            s_ref[0:i * blk, :] = s_past
            m_col = jnp.maximum(m_col, jnp.max(s_past, axis=0, keepdims=True))
        return m_col

    def weights(i, s_ref, m_col, e_ref):
        keys = slice(0, (i + 1) * blk)
        e_ref[keys, :] = jnp.exp2(s_ref[keys, :] - m_col).astype(BF16)

    def finish(i, h, e_ref):
        keys = slice(0, (i + 1) * blk)
        o = jnp.dot(vaug_t_ref[h, :, keys], e_ref[keys, :], preferred_element_type=F32)
        o = o[0:dh, :] / o[dh:dh + 1, :]
        return o * lax.rsqrt(jnp.mean(o * o, axis=0, keepdims=True) + NORM_EPS)

    units = [(i, h) for i in range(nb) for h in range(MOBA_HEADS)]
    s_bufs = (s_a_ref, s_b_ref)
    e_bufs = (e_a_ref, e_b_ref)
    m_cols = {0: scores(*units[0], s_bufs[0]), 1: scores(*units[1], s_bufs[1])}
    weights(units[0][0], s_bufs[0], m_cols.pop(0), e_bufs[0])
    outs = []
    for u, (i, h) in enumerate(units):
        if u + 2 < len(units):
            m_cols[u + 2] = scores(*units[u + 2], s_bufs[u % 2])
        if u + 1 < len(units):
            weights(units[u + 1][0], s_bufs[(u + 1) % 2], m_cols.pop(u + 1), e_bufs[(u + 1) % 2])
        outs.append(finish(i, h, e_bufs[u % 2]))
        if h == MOBA_HEADS - 1:
            y = jnp.concatenate(outs, axis=0).T
            out_ref[i * blk:(i + 1) * blk, :] = (y * g_ref[...]).astype(out_ref.dtype)
            outs = []


def _moba(qkv, g, layer, seq):
    t = qkv.shape[0]
    col = lambda c: pl.BlockSpec((seq, MOBA_WIDTH), lambda b: (b, c))
    return pl.pallas_call(
        _row_views(_moba_body, (3,), layer),
        grid=(t // seq,),
        in_specs=[col(0), col(1), col(2), _resident_layer(g, layer)],
        out_specs=pl.BlockSpec((seq, MOBA_WIDTH), lambda b: (b, 0)),
        out_shape=jax.ShapeDtypeStruct((t, MOBA_WIDTH), BF16),
        scratch_shapes=[pltpu.VMEM((MOBA_HEADS, seq, 2 * MOBA_DH), BF16),
                        pltpu.VMEM((MOBA_HEADS, 2 * MOBA_DH, seq), BF16),
                        pltpu.VMEM((MOBA_HEADS, 2 * MOBA_DH, seq), BF16),
                        pltpu.VMEM((seq, MOBA_BLOCK), F32),
                        pltpu.VMEM((seq, MOBA_BLOCK), F32),
                        pltpu.VMEM((seq, MOBA_BLOCK), BF16),
                        pltpu.VMEM((seq, MOBA_BLOCK), BF16)],
        compiler_params=_params(),
        name="moba",
    )(qkv, qkv, qkv, g)


def _split3(x):
    hi = x.astype(BF16)
    r = x - hi.astype(F32)
    mid = r.astype(BF16)
    lo = (r - mid.astype(F32)).astype(BF16)
    return hi, mid, lo


def _gla_body(q_ref, k_ref, v_ref, la_ref, r_ref, g_ref, out_ref, st_ref):
    blk = GLA_BLOCK
    sub = GLA_SUB
    nsub = blk // sub
    dk = GLA_DK
    dv = GLA_DV

    @pl.when(pl.program_id(1) == 0)
    def _():
        st_ref[...] = jnp.zeros(st_ref.shape, F32)

    nblk = q_ref.shape[0] // blk
    ri = lax.broadcasted_iota(jnp.int32, (blk, blk), 0)
    ci = lax.broadcasted_iota(jnp.int32, (blk, blk), 1)
    tril = jnp.where(ci <= ri, 1.0, 0.0).astype(BF16)
    causal = ci <= ri
    b_all = [sum(jnp.dot(tril, part, preferred_element_type=F32)
                 for part in _split3(la_ref[n * blk:(n + 1) * blk, :])) for n in range(nblk)]

    pair = LANES // dk
    lane_head = lax.broadcasted_iota(jnp.int32, (1, LANES), 1) // dk

    def prep(n, p):
        ls = slice(p * LANES, (p + 1) * LANES)
        rows = slice(n * blk, (n + 1) * blk)
        q = q_ref[rows, ls]
        k = k_ref[rows, ls]
        b = b_all[n][:, ls]
        s = [jnp.zeros((1, LANES), F32)] + [b[I * sub - 1:I * sub, :] for I in range(1, nsub)]
        s_row = jnp.concatenate([jnp.broadcast_to(si, (sub, LANES)) for si in s], axis=0)
        b_end = b[blk - 1:blk, :]
        qt = (q * jnp.exp(b - s_row)).astype(BF16)
        q_in = (q * jnp.exp(b)).astype(BF16)
        k_out = (k * jnp.exp(b_end - b)).astype(BF16)
        k_sub = []
        for I in range(nsub):
            nk = (I + 1) * sub
            k_i = (k[0:nk, :] * jnp.exp(s[I] - b[0:nk, :])).astype(BF16)
            if nk < blk:
                k_i = jnp.concatenate([k_i, jnp.zeros((blk - nk, LANES), BF16)], axis=0)
            k_sub.append(k_i)
        heads = []
        for j in range(pair):
            mine = lane_head == j
            qt_h = jnp.where(mine, qt, 0.0)
            a = jnp.concatenate([_nt(qt_h[I * sub:(I + 1) * sub, :], k_sub[I]) for I in range(nsub)], axis=0)
            a = jnp.where(causal, a, 0.0).astype(BF16)
            heads.append((a, jnp.where(mine, q_in, 0.0), mine))
        return heads, k_out, jnp.exp(b_end)

    def finish(n, h, a, q_in, mine, k_out, decay):
        vs = slice(h * dv, (h + 1) * dv)
        rows = slice(n * blk, (n + 1) * blk)
        v = v_ref[rows, vs]
        st = st_ref[h]
        o = jnp.dot(a, v, preferred_element_type=F32) + _nt(q_in, st.astype(BF16))
        v_t = v.astype(F32).T.astype(BF16)
        st_ref[h] = jnp.where(mine, st * decay + jnp.dot(v_t, k_out, preferred_element_type=F32), 0.0)
        o = o * lax.rsqrt(jnp.mean(o * o, axis=-1, keepdims=True) + NORM_EPS) * g_ref[:, vs]
        out_ref[rows, vs] = (o * r_ref[rows, vs].astype(F32)).astype(out_ref.dtype)

    units = [(n, p) for n in range(nblk) for p in range(GLA_HEADS // pair)]
    nxt = prep(*units[0])
    for u, (n, p) in enumerate(units):
        heads, k_out, decay = nxt
        if u + 1 < len(units):
            nxt = prep(*units[u + 1])
        for j, (a, q_in, mine) in enumerate(heads):
            finish(n, p * pair + j, a, q_in, mine, k_out, decay)


_GLA_STEP_BLOCKS = 4


def _gla(gq, gk, gv, la, gr, g, layer, seq):
    t = gq.shape[0]
    rows = GLA_BLOCK * _GLA_STEP_BLOCKS
    nblk = seq // rows
    blk = lambda n: pl.BlockSpec((rows, n), lambda b, i: (b * nblk + i, 0))
    kw = GLA_HEADS * GLA_DK
    return pl.pallas_call(
        _row_views(_gla_body, (5,), layer),
        grid=(t // seq, nblk),
        in_specs=[blk(kw), blk(kw), blk(GLA_WIDTH), blk(kw), blk(GLA_WIDTH), _resident_layer(g, layer)],
        out_specs=blk(GLA_WIDTH),
        out_shape=jax.ShapeDtypeStruct((t, GLA_WIDTH), BF16),
        scratch_shapes=[pltpu.VMEM((GLA_HEADS, GLA_DV, LANES), F32)],
        compiler_params=_params(2),
        name="gla",
    )(gq, gk, gv, la, gr, g)


_FFN_TILE = 512


def _ffn_body(x_ref, ya_ref, yb_ref, yc_ref, wo_ref, g_ref, wup_ref, cw_ref, cb_ref, wdn_ref,
              fg_ref, out_ref, h_ref, acc_ref, ubuf_a, ubuf_b, carry_ref, abuf_ref,
              *, tiles_per_seq, final_norm):
    tm = x_ref.shape[0]
    nch = D_FF // FFN_CHUNK
    halves = FFN_CHUNK // LANES
    taps = FFN_CONV_WIDTH
    o_b = CONV_CH
    o_c = CONV_CH + MOBA_WIDTH

    @pl.when(pl.program_id(0) % tiles_per_seq == 0)
    def _():
        carry_ref[...] = jnp.zeros(carry_ref.shape, F32)

    y = jnp.dot(ya_ref[...], wo_ref[0:o_b, :], preferred_element_type=F32)
    y = y + jnp.dot(yb_ref[...], wo_ref[o_b:o_c, :], preferred_element_type=F32)
    y = y + jnp.dot(yc_ref[...], wo_ref[o_c:, :], preferred_element_type=F32)
    x1 = x_ref[...] + y
    out_ref[...] = x1
    h_ref[...] = _rms(x1, g_ref[...]).astype(BF16)
    acc_ref[...] = jnp.zeros(acc_ref.shape, F32)

    def up(j, ubuf):
        for s in range(2):
            lo = s * D_FF + j * FFN_CHUNK
            u = jnp.dot(h_ref[...], wup_ref[:, lo:lo + FFN_CHUNK], preferred_element_type=F32)
            for c in range(halves):
                ubuf[s, c, SUBLANES:, :] = u[:, c * LANES:(c + 1) * LANES]

    def gate_down(j, ubuf):
        ys = []
        for s in range(2):
            outs = []
            for c in range(halves):
                lo = s * D_FF + j * FFN_CHUNK + c * LANES
                cols = slice(lo, lo + LANES)
                hist = carry_ref.at[s * nch + j, :, c * LANES:(c + 1) * LANES]
                ubuf[s, c, 0:SUBLANES, :] = hist[...]
                y = cb_ref[:, cols]
                for d in range(taps):
                    y = y + cw_ref[taps - 1 - d:taps - d, cols] * ubuf[s, c, pl.ds(SUBLANES - d, tm), :]
                hist[...] = ubuf[s, c, tm:tm + SUBLANES, :]
                outs.append(y)
            ys.append(jnp.concatenate(outs, axis=1))
        val, gate = ys
        a = (val * _sigmoid(val) * gate).astype(BF16)
        if j % 2 == 0 and j + 1 < nch:
            abuf_ref[:, 0:FFN_CHUNK] = a
        elif j % 2 == 1:
            abuf_ref[:, FFN_CHUNK:] = a
            acc_ref[...] += jnp.dot(abuf_ref[...], wdn_ref[(j - 1) * FFN_CHUNK:(j + 1) * FFN_CHUNK, :],
                                    preferred_element_type=F32)
        else:
            acc_ref[...] += jnp.dot(a, wdn_ref[j * FFN_CHUNK:(j + 1) * FFN_CHUNK, :],
                                    preferred_element_type=F32)

    assert nch % 2 == 1
    up(0, ubuf_a)
    for j in range(0, nch - 1, 2):
        up(j + 1, ubuf_b)
        gate_down(j, ubuf_a)
        up(j + 2, ubuf_a)
        gate_down(j + 1, ubuf_b)
    gate_down(nch - 1, ubuf_a)
    y = out_ref[...] + acc_ref[...]
    if final_norm:
        y = _rms(y, fg_ref[...])
    out_ref[...] = y


def _ffn(x2, ya, yb, yc, wo, g, wup, cw, cb, wdn, fg, layer, tm, seq, final_norm):
    t = x2.shape[0]
    row = lambda n: pl.BlockSpec((tm, n), lambda i: (i, 0))
    nch2 = 2 * D_FF // FFN_CHUNK
    body = functools.partial(_ffn_body, tiles_per_seq=seq // tm, final_norm=final_norm)
    body = _row_views(_row_views(body, (10,), 0), (5, 8), layer)
    return pl.pallas_call(
        body,
        grid=(t // tm,),
        in_specs=([row(D_MODEL), row(CONV_CH), row(MOBA_WIDTH), row(GLA_WIDTH)]
                  + [_resident_layer(p, layer) for p in (wo, g, wup, cw, cb, wdn)]
                  + [_resident_layer(fg, 0)]),
        out_specs=row(D_MODEL),
        out_shape=jax.ShapeDtypeStruct((t, D_MODEL), F32),
        scratch_shapes=[pltpu.VMEM((tm, D_MODEL), BF16),
                        pltpu.VMEM((tm, D_MODEL), F32),
                        pltpu.VMEM((2, FFN_CHUNK // LANES, SUBLANES + tm, LANES), F32),
                        pltpu.VMEM((2, FFN_CHUNK // LANES, SUBLANES + tm, LANES), F32),
                        pltpu.VMEM((nch2, SUBLANES, FFN_CHUNK), F32),
                        pltpu.VMEM((tm, 2 * FFN_CHUNK), BF16)],
        compiler_params=_params(),
        name="ffn",
    )(x2, ya, yb, yc, wo, g, wup, cw, cb, wdn, fg)


def _pack_w_in_body(win_t_ref, win_o):
    o_g = _C_R
    o_r = o_g + GLA_GATE_RANK
    win_o[0:o_g, :] = win_t_ref[0:o_g, :].astype(BF16)
    win_o[_C_R:_C_G, :] = win_t_ref[o_r:, :].astype(BF16)
    win_o[_C_G:_C_G + GLA_GATE_RANK, :] = win_t_ref[o_g:o_r, :].astype(BF16)
    win_o[_C_G + GLA_GATE_RANK:, :] = jnp.zeros(
        (_IN_COLS_PAD - _C_G - GLA_GATE_RANK, win_o.shape[1]), BF16)


def _pack_w_in(w_in_t):
    depth, cols, d = w_in_t.shape
    whole = lambda rows: pl.BlockSpec((None, rows, d), lambda l: (l, 0, 0))
    return pl.pallas_call(
        _pack_w_in_body,
        grid=(depth,),
        in_specs=[whole(cols)],
        out_specs=whole(_IN_COLS_PAD),
        out_shape=jax.ShapeDtypeStruct((depth, _IN_COLS_PAD, d), BF16),
        compiler_params=_params(),
        name="w_in_cast",
    )(w_in_t)


def kernel(x, norm_mix_g, w_in, conv_w, conv_b, conv_ln_g, conv_ln_b, moba_out_g,
           gla_gate_w, gla_gate_b, gla_out_g, w_out, norm_ffn_g, ffn_w_up, ffn_conv_w,
           ffn_conv_b, ffn_w_down, final_g):
    bsz, seq, d = x.shape
    depth = w_in.shape[0]
    x2 = x.reshape(bsz * seq, d)
    w_in_p = _pack_w_in(jnp.swapaxes(w_in, 1, 2))
    for l in range(depth):
        res = _in_proj(
            x2, norm_mix_g, w_in_p, l, gla_gate_w, gla_gate_b, conv_w, conv_b,
            conv_ln_g, conv_ln_b, _IN_PROJ_TILE, seq,
            cast=(w_out, ffn_w_up, ffn_w_down) if l == 0 else ())
        ya, qkv, gq, gk, gv, gr, la = res[:_N_IN_PROJ_OUTS]
        if l == 0:
            w_out_p, w_up_p, w_dn_p = res[_N_IN_PROJ_OUTS:]
        yb = _moba(qkv, moba_out_g, l, seq)
        yc = _gla(gq, gk, gv, la, gr, gla_out_g, l, seq)
        x2 = _ffn(x2, ya, yb, yc, w_out_p, norm_ffn_g, w_up_p, ffn_conv_w,
                  ffn_conv_b, w_dn_p, final_g[None], l, _FFN_TILE, seq,
                  final_norm=(l == depth - 1))
    return x2.reshape(bsz, seq, d)
```
